```python
import math
import jax, jax.numpy as jnp
from jax import lax
import numpy as np

D_MODEL = 1024
BATCH = 8
SEQ = 4096
DEPTH = 2

N_A_LAYERS = DEPTH // 2
N_B_LAYERS = DEPTH - N_A_LAYERS

HEAD_DIM = 128
GDN_HEADS = D_MODEL // HEAD_DIM
GDN_DK = HEAD_DIM
GDN_DV = HEAD_DIM
GDN_WIDTH = GDN_HEADS * HEAD_DIM
CONV_K = 4
CHUNK = 64
FOX_HEADS = D_MODEL // HEAD_DIM
FOX_WIDTH = FOX_HEADS * HEAD_DIM
Q_BLOCK = 128
EPS = 1e-6

kernel_name = "yoco_gdn_fox_hybrid"


def rms_norm(x, g):
    xf = x.astype(jnp.float32)
    y = xf * lax.rsqrt(jnp.mean(xf * xf, axis=-1, keepdims=True) + EPS)
    return (y * g.astype(jnp.float32)).astype(x.dtype)


def l2_norm(x):
    xf = x.astype(jnp.float32)
    return (xf * lax.rsqrt(jnp.sum(xf * xf, axis=-1, keepdims=True) + EPS)).astype(x.dtype)


def causal_depthwise_conv(x, w):
    rhs = w[:, None, :].astype(x.dtype)
    return lax.conv_general_dilated(
        x, rhs, window_strides=(1,), padding=((CONV_K - 1, 0),),
        dimension_numbers=("NWC", "WIO", "NWC"), feature_group_count=x.shape[-1])


def chunk_gated_delta_rule(q, k, v, g, beta):
    B, L, H, Dk = q.shape
    Dv = v.shape[-1]
    N = L // CHUNK
    f32 = jnp.float32
    out_dtype = v.dtype

    def chunks(t):
        t = jnp.moveaxis(t.astype(f32), 2, 1)
        return t.reshape((B, H, N, CHUNK) + t.shape[3:])

    qc = chunks(q) * (Dk ** -0.5)
    kc, vc = chunks(k), chunks(v)
    gc, bc = chunks(g), chunks(beta)
    gcum = jnp.cumsum(gc, axis=-1)
    causal = jnp.tril(jnp.ones((CHUNK, CHUNK), dtype=bool))
    strict = jnp.tril(jnp.ones((CHUNK, CHUNK), dtype=bool), -1)
    diff = gcum[..., :, None] - gcum[..., None, :]
    decay = jnp.where(causal, jnp.exp(jnp.where(causal, diff, 0.0)), 0.0)

    kb = kc * bc[..., None]
    a_mat = jnp.where(strict, jnp.einsum('bhncd,bhnsd->bhncs', kb, kc) * decay, 0.0)
    rhs = jnp.concatenate([vc * bc[..., None], kb * jnp.exp(gcum)[..., None]], axis=-1)
    sol = lax.linalg.triangular_solve(a_mat, rhs, left_side=True, lower=True,
                                      unit_diagonal=True)
    u, w = sol[..., :Dv], sol[..., Dv:]
    attn = jnp.einsum('bhncd,bhnsd->bhncs', qc, kc) * decay

    def step(S, inp):
        q_i, k_i, u_i, w_i, g_i, attn_i = inp
        v_new = u_i - jnp.einsum('bhck,bhkv->bhcv', w_i, S)
        o = (jnp.einsum('bhck,bhkv->bhcv', q_i * jnp.exp(g_i)[..., None], S)
             + jnp.einsum('bhcs,bhsv->bhcv', attn_i, v_new))
        g_last = g_i[..., -1]
        k_dec = k_i * jnp.exp(g_last[..., None] - g_i)[..., None]
        S = S * jnp.exp(g_last)[..., None, None] + jnp.einsum('bhck,bhcv->bhkv', k_dec, v_new)
        return S, o

    xs = tuple(jnp.moveaxis(t, 2, 0) for t in (qc, kc, u, w, gcum, attn))
    S0 = jnp.zeros((B, H, Dk, Dv), f32)
    _, o = lax.scan(step, S0, xs)
    o = o.transpose(1, 0, 3, 2, 4).reshape(B, L, H, Dv)
    return o.astype(out_dtype)


def gdn_mixer(h, w_in, conv_w, a_log, dt_bias, o_norm, w_out):
    B, L, _ = h.shape
    W, H = GDN_WIDTH, GDN_HEADS
    proj = h @ w_in
    qkv, z, a, b = jnp.split(proj, [3 * W, 4 * W, 4 * W + H], axis=-1)
    qkv = jax.nn.silu(causal_depthwise_conv(qkv, conv_w))
    q, k, v = jnp.split(qkv, 3, axis=-1)
    q = l2_norm(q.reshape(B, L, H, GDN_DK))
    k = l2_norm(k.reshape(B, L, H, GDN_DK))
    v = v.reshape(B, L, H, GDN_DV)
    beta = jax.nn.sigmoid(b.astype(jnp.float32))
    g = -jnp.exp(a_log.astype(jnp.float32)) * jax.nn.softplus(a.astype(jnp.float32) + dt_bias.astype(jnp.float32))
    o = chunk_gated_delta_rule(q, k, v, g, beta)
    o = rms_norm(o, o_norm) * jax.nn.silu(z.reshape(B, L, H, GDN_DV))
    return o.reshape(B, L, W) @ w_out


def shared_kv(h, kv_norm, kv_w, kv_forget_bias, kv_k_norm):
    B, L, _ = h.shape
    u = rms_norm(h, kv_norm)
    k, v, f = jnp.split(u @ kv_w, [FOX_WIDTH, 2 * FOX_WIDTH], axis=-1)
    k = rms_norm(k.reshape(B, L, FOX_HEADS, HEAD_DIM), kv_k_norm)
    v = v.reshape(B, L, FOX_HEADS, HEAD_DIM)
    log_f = jax.nn.log_sigmoid(f.astype(jnp.float32) + kv_forget_bias.astype(jnp.float32))
    c = jnp.cumsum(log_f, axis=1)
    return k, v, c


def fox_attention(q, k, v, c):
    B, L, H, D = q.shape
    nb = L // Q_BLOCK
    qb = q.reshape(B, nb, Q_BLOCK, H, D).transpose(1, 0, 2, 3, 4)
    cT = jnp.transpose(c, (0, 2, 1))
    cb = cT.reshape(B, H, nb, Q_BLOCK).transpose(2, 0, 1, 3)
    kpos = jnp.arange(L)
    scale = D ** -0.5

    def block(args):
        i, q_i, c_i = args
        s = jnp.einsum('bqhd,bkhd->bhqk', q_i, k, preferred_element_type=jnp.float32) * scale
        s = s + (c_i[..., :, None] - cT[..., None, :])
        qpos = i * Q_BLOCK + jnp.arange(Q_BLOCK)
        s = jnp.where(kpos[None, :] <= qpos[:, None], s, -jnp.inf)
        p = jax.nn.softmax(s, axis=-1)
        return jnp.einsum('bhqk,bkhd->bqhd', p.astype(v.dtype), v)

    o = lax.map(block, (jnp.arange(nb), qb, cb))
    return o.transpose(1, 0, 2, 3, 4).reshape(B, L, H, D)


def fox_mixer(h, k, v, c, w_in, q_norm, w_out):
    B, L, _ = h.shape
    q, z = jnp.split(h @ w_in, [FOX_WIDTH], axis=-1)
    q = rms_norm(q.reshape(B, L, FOX_HEADS, HEAD_DIM), q_norm)
    o = fox_attention(q, k, v, c)
    o = o * jax.nn.silu(z.reshape(B, L, FOX_HEADS, HEAD_DIM))
    return o.reshape(B, L, FOX_WIDTH) @ w_out


def setup_inputs(seed: int = 0) -> dict:
    key = jax.random.key(seed)
    ks = jax.random.split(key, 24)
    D, W, H = D_MODEL, GDN_WIDTH, GDN_HEADS
    nA, nB = N_A_LAYERS, N_B_LAYERS

    def nrm(k, shape, fan_in):
        return jax.random.normal(k, shape, jnp.float32) * (fan_in ** -0.5)

    def gain(k, shape):
        return 1.0 + 0.05 * jax.random.normal(k, shape, jnp.float32)

    dt = jnp.exp(jax.random.uniform(ks[5], (nA, H), jnp.float32, math.log(1e-3), math.log(1e-1)))
    return {
        "x": jax.random.normal(ks[0], (BATCH, SEQ, D), jnp.float32),
        "gdn_pre_norm": gain(ks[1], (nA, D)),
        "gdn_w_in": nrm(ks[2], (nA, D, 4 * W + 2 * H), D),
        "gdn_conv_w": nrm(ks[3], (nA, CONV_K, 3 * W), CONV_K),
        "gdn_a_log": jnp.log(jax.random.uniform(ks[4], (nA, H), jnp.float32, 1.0, 16.0)),
        "gdn_dt_bias": dt + jnp.log(-jnp.expm1(-dt)),
        "gdn_o_norm": gain(ks[6], (nA, GDN_DV)),
        "gdn_w_out": nrm(ks[7], (nA, W, D), W),
        "gdn_post_norm": gain(ks[8], (nA, D)),
        "kv_norm": gain(ks[9], (D,)),
        "kv_w": nrm(ks[10], (D, 2 * FOX_WIDTH + FOX_HEADS), D),
        "kv_forget_bias": 1.0 + 0.5 * jax.random.normal(ks[11], (FOX_HEADS,), jnp.float32),
        "kv_k_norm": gain(ks[12], (HEAD_DIM,)),
        "fox_pre_norm": gain(ks[13], (nB, D)),
        "fox_w_in": nrm(ks[14], (nB, D, 2 * FOX_WIDTH), D),
        "fox_q_norm": gain(ks[15], (nB, HEAD_DIM)),
        "fox_w_out": nrm(ks[16], (nB, FOX_WIDTH, D), FOX_WIDTH),
        "fox_post_norm": gain(ks[17], (nB, D)),
    }


def reference(x, gdn_pre_norm, gdn_w_in, gdn_conv_w, gdn_a_log, gdn_dt_bias, gdn_o_norm,
              gdn_w_out, gdn_post_norm, kv_norm, kv_w, kv_forget_bias, kv_k_norm,
              fox_pre_norm, fox_w_in, fox_q_norm, fox_w_out, fox_post_norm):
    h = x
    k_sh = v_sh = c_sh = None
    for layer in range(DEPTH):
        if layer < N_A_LAYERS:
            i = layer
            y = gdn_mixer(rms_norm(h, gdn_pre_norm[i]), gdn_w_in[i], gdn_conv_w[i],
                          gdn_a_log[i], gdn_dt_bias[i], gdn_o_norm[i], gdn_w_out[i])
            h = h + rms_norm(y, gdn_post_norm[i])
        else:
            if layer == N_A_LAYERS:
                k_sh, v_sh, c_sh = shared_kv(h, kv_norm, kv_w, kv_forget_bias, kv_k_norm)
            i = layer - N_A_LAYERS
            y = fox_mixer(rms_norm(h, fox_pre_norm[i]), k_sh, v_sh, c_sh,
                          fox_w_in[i], fox_q_norm[i], fox_w_out[i])
            h = h + rms_norm(y, fox_post_norm[i])
    return h
```

```python
import functools
import math

import jax
import jax.numpy as jnp
import numpy as np
from jax import lax
from jax.experimental import pallas as pl
from jax.experimental.pallas import tpu as pltpu

HEAD_DIM = 128
N_HEADS = 8
CONV_K = 4
CHUNK = 64
GROUP = 256
EPS = 1e-6
LANES = 128
VMEM_LIMIT_BYTES = 56 * 1024 * 1024
NEG_BIG = -1e30

F32 = jnp.float32
BF16 = jnp.bfloat16


def _dot(a, b):
    return jnp.dot(a, b, preferred_element_type=F32)


def _dot_nt(a, b):
    return lax.dot_general(a, b, (((1,), (1,)), ((), ())), preferred_element_type=F32)


def _dot_tn(a, b):
    return lax.dot_general(a, b, (((0,), (0,)), ((), ())), preferred_element_type=F32)


def _split3(x):
    hi = x.astype(BF16)
    r1 = x - hi.astype(F32)
    mid = r1.astype(BF16)
    lo = (r1 - mid.astype(F32)).astype(BF16)
    return hi, mid, lo


def _dot3(a_bf16, x):
    hi, mid, lo = _split3(x)
    return _dot(a_bf16, hi) + _dot(a_bf16, mid) + _dot(a_bf16, lo)


def _dot3_r(x, b_bf16):
    hi, mid, lo = _split3(x)
    return _dot(hi, b_bf16) + _dot(mid, b_bf16) + _dot(lo, b_bf16)


def _sigmoid(x):
    return 1.0 / (1.0 + jnp.exp(-x))


def _softplus(x):
    return jnp.maximum(x, 0.0) + jnp.log(1.0 + jnp.exp(-jnp.abs(x)))


def _rms(x, gain):
    ms = jnp.mean(x * x, axis=-1, keepdims=True)
    return x * lax.rsqrt(ms + EPS) * gain


def _resident(shape):
    nd = len(shape)
    return pl.BlockSpec(shape, lambda *_: (0,) * nd, pipeline_mode=pl.Buffered(1))


def _gdn_in_kernel(x_ref, gain_ref, w_ref, wab_ref, wabt_ref, convw_ref, hpc_ref, hpr_ref,
                   lt_ref, ut_ref,
                   q_ref, k_ref, v_ref, z_ref, gcol_ref, grow_ref,
                   xn_scr, pre_scr, carry_scr, *, tm, width, col_chunk):
    @pl.when(pl.program_id(1) == 0)
    def _():
        carry_scr[...] = jnp.zeros_like(carry_scr)

    xn_scr[...] = _rms(x_ref[0], gain_ref[...]).astype(BF16)
    xn = xn_scr[...]
    q_scale = HEAD_DIM ** -0.5

    n_qkv = 3 * width // col_chunk
    for c in range(n_qkv):
        lo = c * col_chunk
        pc = _dot(xn, w_ref[:, lo:lo + col_chunk])
        pre_scr[0:8, :] = carry_scr[:, lo:lo + col_chunk]
        pre_scr[8:8 + tm, :] = pc
        carry_scr[:, lo:lo + col_chunk] = pc[tm - 8:tm, :]
        cw = convw_ref[:, lo:lo + col_chunk]
        acc = pc * cw[CONV_K - 1:CONV_K, :]
        for j in range(1, CONV_K):
            acc = acc + pre_scr[8 - j:8 - j + tm, :] * cw[CONV_K - 1 - j:CONV_K - j, :]
        y = acc * _sigmoid(acc)
        for hh in range(col_chunk // HEAD_DIM):
            col = lo + hh * HEAD_DIM
            yh = y[:, hh * HEAD_DIM:(hh + 1) * HEAD_DIM]
            if col < 2 * width:
                ss = jnp.sum(yh * yh, axis=-1, keepdims=True)
                yh = yh * lax.rsqrt(ss + EPS)
            head = (col % width) // HEAD_DIM
            if col < width:
                q_ref[0, head] = (yh * q_scale).astype(BF16)
            elif col < 2 * width:
                k_ref[0, head] = yh.astype(BF16)
            else:
                v_ref[0, head] = yh.astype(BF16)
    for c in range(width // col_chunk):
        lo = 3 * width + c * col_chunk
        z_ref[0, :, c * col_chunk:(c + 1) * col_chunk] = _dot(xn, w_ref[:, lo:lo + col_chunk]).astype(BF16)

    ab = _dot(xn, wab_ref[...])
    lane = lax.broadcasted_iota(jnp.int32, ab.shape, 1)
    g_col = jnp.where(lane < N_HEADS, hpc_ref[0:1, :] * _softplus(ab + hpc_ref[1:2, :]), 0.0)
    beta_col = _sigmoid(ab)
    cums = _dot3(lt_ref[...], g_col)
    gc_col = cums[0:tm]
    gl_col = pltpu.roll(cums[tm:2 * tm], 2 * N_HEADS, axis=1)
    gcol_ref[0] = jnp.where(lane < N_HEADS, gc_col,
                            jnp.where(lane < 2 * N_HEADS, beta_col, gl_col))

    abt = _dot_nt(wabt_ref[...], xn)
    row = lax.broadcasted_iota(jnp.int32, abt.shape, 0)
    g_row = jnp.where(row < N_HEADS, hpr_ref[0] * _softplus(abt + hpr_ref[1]), 0.0)
    grow_ref[0] = _dot3_r(g_row, ut_ref[...])


def _chunk_tri(tm):
    t = np.arange(tm)
    same = (t[:, None] // CHUNK) == (t[None, :] // CHUNK)
    lower = same & (t[None, :] <= t[:, None])
    return lower.astype(np.float32), same.astype(np.float32)


def _gdn_in(x, pre_gain, w_in, conv_w, a_log, dt_bias, *, tm=256, col_chunk=512):
    B, L, D = x.shape
    W = N_HEADS * HEAD_DIM
    H = N_HEADS
    w_main = w_in[:, :4 * W].astype(BF16)
    wab = jnp.zeros((D, LANES), F32).at[:, :2 * H].set(w_in[:, 4 * W:4 * W + 2 * H]).astype(BF16)
    wabt = w_in[:, 4 * W:4 * W + 2 * H].T.astype(BF16)
    neg_a = -jnp.exp(a_log.astype(F32))
    hpc = jnp.zeros((2, LANES), F32).at[0, :H].set(neg_a).at[1, :H].set(dt_bias.astype(F32))
    hpr = jnp.zeros((2, 2 * H, tm), F32)
    hpr = hpr.at[0, :H, :].set(jnp.broadcast_to(neg_a[:, None], (H, tm)))
    hpr = hpr.at[1, :H, :].set(jnp.broadcast_to(dt_bias.astype(F32)[:, None], (H, tm)))
    lower, same = _chunk_tri(tm)
    lt = jnp.asarray(np.concatenate([lower, same], axis=0), BF16)
    ut = jnp.asarray(lower.T, BF16)

    kern = functools.partial(_gdn_in_kernel, tm=tm, width=W, col_chunk=col_chunk)
    row_spec = lambda n: pl.BlockSpec((1, tm, n), lambda b, i: (b, i, 0))
    head_spec = pl.BlockSpec((1, H, tm, HEAD_DIM), lambda b, i: (b, 0, i, 0))
    out_shape = (
        jax.ShapeDtypeStruct((B, H, L, HEAD_DIM), BF16),
        jax.ShapeDtypeStruct((B, H, L, HEAD_DIM), BF16),
        jax.ShapeDtypeStruct((B, H, L, HEAD_DIM), BF16),
        jax.ShapeDtypeStruct((B, L, W), BF16),
        jax.ShapeDtypeStruct((B, L, LANES), F32),
        jax.ShapeDtypeStruct((B, 2 * H, L), F32),
    )
    return pl.pallas_call(
        kern,
        grid=(B, L // tm),
        in_specs=[
            row_spec(D),
            _resident((1, D)),
            _resident((D, 4 * W)),
            _resident((D, LANES)),
            _resident((2 * H, D)),
            _resident((CONV_K, 3 * W)),
            _resident((2, LANES)),
            _resident((2, 2 * H, tm)),
            _resident((2 * tm, tm)),
            _resident((tm, tm)),
        ],
        out_specs=(head_spec, head_spec, head_spec, row_spec(W), row_spec(LANES),
                   pl.BlockSpec((1, 2 * H, tm), lambda b, i: (b, 0, i))),
        out_shape=out_shape,
        scratch_shapes=[
            pltpu.VMEM((tm, D), BF16),
            pltpu.VMEM((tm + 8, col_chunk), F32),
            pltpu.VMEM((8, 3 * W), F32),
        ],
        compiler_params=pltpu.CompilerParams(
            dimension_semantics=("arbitrary", "arbitrary"),
            vmem_limit_bytes=VMEM_LIMIT_BYTES),
        name="gdn_in",
    )(x, pre_gain.reshape(1, D).astype(F32), w_main, wab, wabt, conv_w.astype(F32), hpc, hpr, lt, ut)


def _gdn_scan_kernel(q_ref, k_ref, v_ref, gcol_ref, grow_ref, cmask_ref, smask_ref, eye_ref,
                     o_ref, bc_scr, s_scr, *, heads_per_iter):
    R = GROUP
    nc = R // CHUNK

    @pl.when(pl.program_id(1) == 0)
    def _():
        s_scr[...] = jnp.zeros_like(s_scr)

    gcol = gcol_ref[0]
    for h in range(N_HEADS):
        for t in range(3):
            col = gcol[:, t * N_HEADS + h:t * N_HEADS + h + 1]
            bc_scr[h, t] = jnp.broadcast_to(col, (R, LANES))

    def one_head(h):
        qh = q_ref[0, h]
        kh = k_ref[0, h]
        vh = v_ref[0, h]
        gcb = bc_scr[h, 0]
        betab = bc_scr[h, 1]
        glb = bc_scr[h, 2]
        g_row = grow_ref[0, pl.ds(h, 1), :]
        kf = kh.astype(F32)
        kb = kf * betab
        egc = jnp.exp(gcb)
        kbg = (kb * egc).astype(BF16)
        vb = (vh.astype(F32) * betab).astype(BF16)
        qg = (qh.astype(F32) * egc).astype(BF16)
        kdec = (kf * jnp.exp(glb - gcb)).astype(BF16)

        kk = _dot_nt(kb.astype(BF16), kh)
        qk = _dot_nt(qh, kh)
        diff = jnp.concatenate([gcb, gcb], axis=1) - g_row
        dec = jnp.exp(jnp.where(cmask_ref[...] > 0, diff, NEG_BIG))
        attn = (qk * dec).astype(BF16)
        x = -(kk * dec * smask_ref[...])
        p = eye_ref[...] + x
        for _ in range(5):
            xb = x.astype(BF16)
            x = _dot(xb, xb)
            p = p + _dot(p.astype(BF16), x.astype(BF16))
        uw = _dot(p.astype(BF16), jnp.concatenate([vb, kbg], axis=1))
        u = uw[:, :HEAD_DIM]
        wm = uw[:, HEAD_DIM:].astype(BF16)

        s = s_scr[h]
        vn_parts = []
        for c in range(nc):
            r0 = c * CHUNK
            lhs = jnp.concatenate([wm[r0:r0 + CHUNK], qg[r0:r0 + CHUNK]], axis=0)
            wq = _dot(lhs, s.astype(BF16))
            vn = (u[r0:r0 + CHUNK] - wq[:CHUNK]).astype(BF16)
            vn_parts.append(vn)
            pad = [jnp.zeros(((nc - 1 - c) * CHUNK, HEAD_DIM), BF16)] if c < nc - 1 else []
            vn_all = jnp.concatenate(vn_parts + pad, axis=0)
            o_c = wq[CHUNK:] + _dot(attn[r0:r0 + CHUNK, :], vn_all)
            o_ref[0, h, r0:r0 + CHUNK, :] = o_c.astype(o_ref.dtype)
            s = s * jnp.exp(glb[r0:r0 + 1, :]) + _dot_tn(kdec[r0:r0 + CHUNK], vn)
        s_scr[h] = s

    def body(it, carry):
        for j in range(heads_per_iter):
            one_head(it * heads_per_iter + j)
        return carry

    lax.fori_loop(0, N_HEADS // heads_per_iter, body, 0)


def _group_masks():
    t = np.arange(GROUP)
    same = (t[:, None] // CHUNK) == (t[None, :] // CHUNK)
    causal = same & (t[None, :] <= t[:, None])
    strict = same & (t[None, :] < t[:, None])
    return causal.astype(np.float32), strict.astype(np.float32), np.eye(GROUP, dtype=np.float32)


def _gdn_scan(q, k, v, gcol, grow, *, heads_per_iter=2):
    B, H, L, Dh = q.shape
    causal, strict, eye = _group_masks()
    head_spec = pl.BlockSpec((1, H, GROUP, Dh), lambda b, g: (b, 0, g, 0))
    kern = functools.partial(_gdn_scan_kernel, heads_per_iter=heads_per_iter)
    return pl.pallas_call(
        kern,
        grid=(B, L // GROUP),
        in_specs=[
            head_spec, head_spec, head_spec,
            pl.BlockSpec((1, GROUP, LANES), lambda b, g: (b, g, 0)),
            pl.BlockSpec((1, 2 * H, GROUP), lambda b, g: (b, 0, g)),
            _resident((GROUP, GROUP)), _resident((GROUP, GROUP)), _resident((GROUP, GROUP)),
        ],
        out_specs=head_spec,
        out_shape=jax.ShapeDtypeStruct((B, H, L, Dh), BF16),
        scratch_shapes=[
            pltpu.VMEM((H, 3, GROUP, LANES), F32),
            pltpu.VMEM((H, Dh, Dh), F32),
        ],
        compiler_params=pltpu.CompilerParams(
            dimension_semantics=("arbitrary", "arbitrary"),
            vmem_limit_bytes=VMEM_LIMIT_BYTES),
        name="gdn_scan",
    )(q, k, v, gcol, grow, jnp.asarray(causal), jnp.asarray(strict), jnp.asarray(eye))


def _head_rms(xh, gain):
    ms = jnp.mean(xh * xh, axis=-1, keepdims=True)
    return xh * lax.rsqrt(ms + EPS) * gain


def _mid_kernel(o_ref, z_ref, x_ref, onorm_ref, wo_ref, post_ref,
                kvn_ref, wk_ref, wvt_ref, wf_ref, fb_ref, kkn_ref,
                fpre_ref, wq_ref, wz_ref, qn_ref, lt_ref, selq_ref, selk_ref, qc_ref, kc_ref,
                h1_ref, k_ref, kaug_ref, vt_ref, q_ref, qaug_ref, z2_ref,
                carry_scr, *, tm):
    @pl.when(pl.program_id(1) == 0)
    def _():
        carry_scr[...] = jnp.zeros_like(carry_scr)

    gated = []
    for h in range(N_HEADS):
        on = _head_rms(o_ref[0, h].astype(F32), onorm_ref[...])
        zh = z_ref[0, :, h * HEAD_DIM:(h + 1) * HEAD_DIM].astype(F32)
        gated.append((on * (zh * _sigmoid(zh))).astype(BF16))
    y = _dot(jnp.concatenate(gated, axis=1), wo_ref[...])
    h1 = x_ref[0] + _rms(y, post_ref[...])
    h1_ref[0] = h1

    u = _rms(h1, kvn_ref[...]).astype(BF16)
    kk = _dot(u, wk_ref[...])
    for h in range(N_HEADS):
        kh = _head_rms(kk[:, h * HEAD_DIM:(h + 1) * HEAD_DIM], kkn_ref[...])
        k_ref[0, h] = kh.astype(BF16)
    vt_ref[0, 0] = _dot_nt(wvt_ref[...], u).astype(BF16)
    f = _dot(u, wf_ref[...]) + fb_ref[...]
    lane = lax.broadcasted_iota(jnp.int32, f.shape, 1)
    logf = jnp.where(lane < N_HEADS, -_softplus(-f), 0.0)
    c = _dot3(lt_ref[...], logf) + carry_scr[0:1, :]
    carry_scr[...] = jnp.broadcast_to(c[tm - 1:tm, :], carry_scr.shape)
    c_hi, c_mid, c_lo = _split3(c)
    qaug = (_dot(c_hi, selq_ref[0]) + _dot(c_mid, selq_ref[1]) + _dot(c_lo, selq_ref[2]) + qc_ref[...])
    kaug = (_dot(c_hi, selk_ref[0]) + _dot(c_mid, selk_ref[1]) + _dot(c_lo, selk_ref[2]) + kc_ref[...])
    for h in range(N_HEADS):
        qaug_ref[0, h] = qaug[:, h * HEAD_DIM:(h + 1) * HEAD_DIM].astype(BF16)
        kaug_ref[0, h] = kaug[:, h * HEAD_DIM:(h + 1) * HEAD_DIM].astype(BF16)

    u2 = _rms(h1, fpre_ref[...]).astype(BF16)
    qq = _dot(u2, wq_ref[...])
    q_scale = HEAD_DIM ** -0.5
    for h in range(N_HEADS):
        qh = _head_rms(qq[:, h * HEAD_DIM:(h + 1) * HEAD_DIM], qn_ref[...])
        q_ref[0, h] = (qh * q_scale).astype(BF16)
    z2_ref[0] = _dot(u2, wz_ref[...]).astype(BF16)


def _aug_selectors():
    selq = np.zeros((3, LANES, N_HEADS * HEAD_DIM), np.float32)
    selk = np.zeros((3, LANES, N_HEADS * HEAD_DIM), np.float32)
    qc = np.zeros((1, N_HEADS * HEAD_DIM), np.float32)
    kc = np.zeros((1, N_HEADS * HEAD_DIM), np.float32)
    for h in range(N_HEADS):
        for t in range(3):
            selq[t, h, h * HEAD_DIM + t] = 1.0
            selk[t, h, h * HEAD_DIM + 3 + t] = -1.0
            qc[0, h * HEAD_DIM + 3 + t] = 1.0
            kc[0, h * HEAD_DIM + t] = 1.0
    return selq, selk, qc, kc


def _mid(o, z, x, o_norm, w_out, post_gain, kv_norm, kv_w, kv_fbias, kv_k_norm,
         fox_pre, fox_w_in, fox_q_norm, *, tm=256):
    B, L, D = x.shape
    H, W = N_HEADS, N_HEADS * HEAD_DIM
    wo = w_out.astype(BF16)
    wk = kv_w[:, :W].astype(BF16)
    wvt = kv_w[:, W:2 * W].T.astype(BF16)
    wf = jnp.zeros((D, LANES), F32).at[:, :H].set(kv_w[:, 2 * W:2 * W + H]).astype(BF16)
    fb = jnp.zeros((1, LANES), F32).at[0, :H].set(kv_fbias.astype(F32))
    wq = fox_w_in[:, :W].astype(BF16)
    wz = fox_w_in[:, W:2 * W].astype(BF16)
    lt = jnp.asarray(np.tril(np.ones((tm, tm), np.float32)), BF16)
    selq, selk, qc, kc = _aug_selectors()
    row = lambda n: pl.BlockSpec((1, tm, n), lambda b, i: (b, i, 0))
    head_spec = pl.BlockSpec((1, H, tm, HEAD_DIM), lambda b, i: (b, 0, i, 0))
    vec = lambda a, n: a.reshape(1, n).astype(F32)
    head_out = jax.ShapeDtypeStruct((B, H, L, HEAD_DIM), BF16)
    out_shape = (
        jax.ShapeDtypeStruct((B, L, D), F32),
        head_out, head_out,
        jax.ShapeDtypeStruct((B, L // tm, W, tm), BF16),
        head_out, head_out,
        jax.ShapeDtypeStruct((B, L, W), BF16),
    )
    return pl.pallas_call(
        functools.partial(_mid_kernel, tm=tm),
        grid=(B, L // tm),
        in_specs=[
            head_spec, row(W), row(D),
            _resident((1, HEAD_DIM)), _resident((W, D)), _resident((1, D)),
            _resident((1, D)), _resident((D, W)), _resident((W, D)), _resident((D, LANES)),
            _resident((1, LANES)), _resident((1, HEAD_DIM)),
            _resident((1, D)), _resident((D, W)), _resident((D, W)), _resident((1, HEAD_DIM)),
            _resident((tm, tm)), _resident((3, LANES, W)), _resident((3, LANES, W)),
            _resident((1, W)), _resident((1, W)),
        ],
        out_specs=(row(D), head_spec, head_spec,
                   pl.BlockSpec((1, 1, W, tm), lambda b, i: (b, i, 0, 0)),
                   head_spec, head_spec, row(W)),
        out_shape=out_shape,
        scratch_shapes=[pltpu.VMEM((8, LANES), F32)],
        compiler_params=pltpu.CompilerParams(
            dimension_semantics=("arbitrary", "arbitrary"),
            vmem_limit_bytes=VMEM_LIMIT_BYTES),
        name="mid",
    )(o, z, x, vec(o_norm, HEAD_DIM), wo, vec(post_gain, D),
      vec(kv_norm, D), wk, wvt, wf, fb, vec(kv_k_norm, HEAD_DIM),
      vec(fox_pre, D), wq, wz, vec(fox_q_norm, HEAD_DIM),
      lt, jnp.asarray(selq, BF16), jnp.asarray(selk, BF16), jnp.asarray(qc), jnp.asarray(kc))


def _fox_attn_kernel(q_ref, qaug_ref, k_ref, kaug_ref, vt_ref, o_ref, *, tq, tk):
    i = pl.program_id(2)
    q = q_ref[0, 0]
    qa = qaug_ref[0, 0]
    n_kv = (i + 1) * (tq // tk)
    key_iota = lax.broadcasted_iota(jnp.int32, (tk, tq), 0)
    qry_pos = i * tq + lax.broadcasted_iota(jnp.int32, (tk, tq), 1)

    def body(j, carry):
        m, l, acc = carry
        r0 = pl.multiple_of(j * tk, tk)
        kj = k_ref[0, 0, pl.ds(r0, tk), :]
        kaj = kaug_ref[0, 0, pl.ds(r0, tk), :]
        s = _dot_nt(kj, q) + _dot_nt(kaj, qa)
        s = jnp.where(key_iota + j * tk <= qry_pos, s, NEG_BIG)
        m_new = jnp.maximum(m, jnp.max(s, axis=0, keepdims=True))
        alpha = jnp.exp(m - m_new)
        p = jnp.exp(s - m_new)
        l = alpha * l + jnp.sum(p, axis=0, keepdims=True)
        acc = alpha * acc + _dot(vt_ref[0, j], p.astype(BF16))
        return m_new, l, acc

    init = (jnp.full((1, tq), NEG_BIG, F32), jnp.zeros((1, tq), F32), jnp.zeros((HEAD_DIM, tq), F32))
    m, l, acc = lax.fori_loop(0, n_kv, body, init)
    o_ref[0, 0] = (acc / l).T.astype(o_ref.dtype)


def _fox_attn(q, qaug, k, kaug, vt, *, tq=512):
    B, H, L, Dh = q.shape
    tk = vt.shape[-1]
    qspec = pl.BlockSpec((1, 1, tq, Dh), lambda b, h, i: (b, h, i, 0))
    kspec = pl.BlockSpec((1, 1, L, Dh), lambda b, h, i: (b, h, 0, 0))
    return pl.pallas_call(
        functools.partial(_fox_attn_kernel, tq=tq, tk=tk),
        grid=(B, H, L // tq),
        in_specs=[qspec, qspec, kspec, kspec,
                  pl.BlockSpec((1, L // tk, Dh, tk), lambda b, h, i: (b, 0, h, 0))],
        out_specs=qspec,
        out_shape=jax.ShapeDtypeStruct((B, H, L, Dh), BF16),
        compiler_params=pltpu.CompilerParams(
            dimension_semantics=("arbitrary", "arbitrary", "arbitrary"),
            vmem_limit_bytes=VMEM_LIMIT_BYTES),
        name="fox_attn",
    )(q, qaug, k, kaug, vt)


def _fox_out_kernel(o_ref, z_ref, h_ref, wo_ref, post_ref, out_ref):
    gated = []
    for h in range(N_HEADS):
        zh = z_ref[0, :, h * HEAD_DIM:(h + 1) * HEAD_DIM].astype(F32)
        gated.append((o_ref[0, h].astype(F32) * (zh * _sigmoid(zh))).astype(BF16))
    y = _dot(jnp.concatenate(gated, axis=1), wo_ref[...])
    out_ref[0] = h_ref[0] + _rms(y, post_ref[...])


def _fox_out(o, z2, h1, w_out, post_gain, *, tm=256):
    B, L, D = h1.shape
    H, W = N_HEADS, N_HEADS * HEAD_DIM
    row = lambda n: pl.BlockSpec((1, tm, n), lambda b, i: (b, i, 0))
    return pl.pallas_call(
        _fox_out_kernel,
        grid=(B, L // tm),
        in_specs=[pl.BlockSpec((1, H, tm, HEAD_DIM), lambda b, i: (b, 0, i, 0)), row(W), row(D),
                  _resident((W, D)), _resident((1, D))],
        out_specs=row(D),
        out_shape=jax.ShapeDtypeStruct((B, L, D), F32),
        compiler_params=pltpu.CompilerParams(
            dimension_semantics=("arbitrary", "arbitrary"),
            vmem_limit_bytes=VMEM_LIMIT_BYTES),
        name="fox_out",
    )(o, z2, h1, w_out.astype(BF16), post_gain.reshape(1, D).astype(F32))


def kernel(x, gdn_pre_norm, gdn_w_in, gdn_conv_w, gdn_a_log, gdn_dt_bias, gdn_o_norm, gdn_w_out, gdn_post_norm, kv_norm, kv_w, kv_forget_bias, kv_k_norm, fox_pre_norm, fox_w_in, fox_q_norm, fox_w_out, fox_post_norm):
    assert gdn_w_in.shape[0] == 1 and fox_w_in.shape[0] == 1, "one GDN layer then one FoX layer"
    q, k, v, z, gcol, grow = _gdn_in(x, gdn_pre_norm[0], gdn_w_in[0], gdn_conv_w[0],
                                     gdn_a_log[0], gdn_dt_bias[0])
    o = _gdn_scan(q, k, v, gcol, grow)
    h1, ks, kaug, vt, qs, qaug, z2 = _mid(
        o, z, x, gdn_o_norm[0], gdn_w_out[0], gdn_post_norm[0],
        kv_norm, kv_w, kv_forget_bias, kv_k_norm, fox_pre_norm[0], fox_w_in[0], fox_q_norm[0])
    o2 = _fox_attn(qs, qaug, ks, kaug, vt)
    return _fox_out(o2, z2, h1, fox_w_out[0], fox_post_norm[0])
```

```python
import functools
import math

import jax
import jax.numpy as jnp
import numpy as np
from jax import lax
from jax.experimental import pallas as pl
from jax.experimental.pallas import tpu as pltpu

HEAD_DIM = 128
N_HEADS = 8
CONV_K = 4
CHUNK = 64
GROUP = 256
SCAN_HP = 8
SLAB = 256
LOG2E = math.log2(math.e)
EPS = 1e-6
LANES = 128
VMEM_LIMIT_BYTES = 56 * 1024 * 1024
NEG_BIG = -1e30

F32 = jnp.float32
BF16 = jnp.bfloat16


def _dot(a, b):
    return jnp.dot(a, b, preferred_element_type=F32)


def _dot_nt(a, b):
    return lax.dot_general(a, b, (((1,), (1,)), ((), ())), preferred_element_type=F32)


def _dot_tn(a, b):
    return lax.dot_general(a, b, (((0,), (0,)), ((), ())), preferred_element_type=F32)


def _split3(x):
    hi = x.astype(BF16)
    r1 = x - hi.astype(F32)
    mid = r1.astype(BF16)
    lo = (r1 - mid.astype(F32)).astype(BF16)
    return hi, mid, lo


def _dot3(a_bf16, x):
    hi, mid, lo = _split3(x)
    return _dot(a_bf16, hi) + _dot(a_bf16, mid) + _dot(a_bf16, lo)


def _dot3_r(x, b_bf16):
    hi, mid, lo = _split3(x)
    return _dot(hi, b_bf16) + _dot(mid, b_bf16) + _dot(lo, b_bf16)


def _sigmoid(x):
    return 1.0 / (1.0 + jnp.exp(-x))


def _softplus(x):
    return jnp.maximum(x, 0.0) + jnp.log(1.0 + jnp.exp(-jnp.abs(x)))


def _rms(x, gain):
    ms = jnp.mean(x * x, axis=-1, keepdims=True)
    return x * lax.rsqrt(ms + EPS) * gain


def _resident(shape):
    nd = len(shape)
    return pl.BlockSpec(shape, lambda *_: (0,) * nd, pipeline_mode=pl.Buffered(1))


def _gdn_in_kernel(x_ref, gain_ref, w_ref, wab_ref, wabt_ref, convw_ref, hpc_ref, hpr_ref,
                   lt_ref, ut_ref,
                   q_ref, k_ref, v_ref, z_ref, gcol_ref, grow_ref,
                   xn_scr, pre_scr, carry_scr, *, tm, width, col_chunk):
    @pl.when(pl.program_id(1) == 0)
    def _():
        carry_scr[...] = jnp.zeros_like(carry_scr)

    xn_scr[...] = _rms(x_ref[0], gain_ref[...]).astype(BF16)
    xn = xn_scr[...]
    q_scale = HEAD_DIM ** -0.5

    n_qkv = 3 * width // col_chunk
    for c in range(n_qkv):
        lo = c * col_chunk
        pc = _dot(xn, w_ref[:, lo:lo + col_chunk])
        pre_scr[0:8, :] = carry_scr[:, lo:lo + col_chunk]
        pre_scr[8:8 + tm, :] = pc
        carry_scr[:, lo:lo + col_chunk] = pc[tm - 8:tm, :]
        cw = convw_ref[:, lo:lo + col_chunk]
        acc = pc * cw[CONV_K - 1:CONV_K, :]
        for j in range(1, CONV_K):
            acc = acc + pre_scr[8 - j:8 - j + tm, :] * cw[CONV_K - 1 - j:CONV_K - j, :]
        y = acc * _sigmoid(acc)
        for hh in range(col_chunk // HEAD_DIM):
            col = lo + hh * HEAD_DIM
            yh = y[:, hh * HEAD_DIM:(hh + 1) * HEAD_DIM]
            if col < 2 * width:
                ss = jnp.sum(yh * yh, axis=-1, keepdims=True)
                yh = yh * lax.rsqrt(ss + EPS)
            head = (col % width) // HEAD_DIM
            if col < width:
                q_ref[0, head] = (yh * q_scale).astype(BF16)
            elif col < 2 * width:
                k_ref[0, head] = yh.astype(BF16)
            else:
                v_ref[0, head] = yh.astype(BF16)
    for c in range(width // col_chunk):
        lo = 3 * width + c * col_chunk
        z_ref[0, :, c * col_chunk:(c + 1) * col_chunk] = _dot(xn, w_ref[:, lo:lo + col_chunk]).astype(BF16)

    ab = _dot(xn, wab_ref[...])
    lane = lax.broadcasted_iota(jnp.int32, ab.shape, 1)
    g_col = jnp.where(lane < N_HEADS, hpc_ref[0:1, :] * _softplus(ab + hpc_ref[1:2, :]), 0.0)
    beta_col = _sigmoid(ab)
    cums = _dot3(lt_ref[...], g_col)
    gc_col = cums[0:tm]
    gl_col = pltpu.roll(cums[tm:2 * tm], 2 * N_HEADS, axis=1)
    gates = jnp.where(lane < N_HEADS, gc_col, jnp.where(lane < 2 * N_HEADS, beta_col, gl_col))
    for hg in range(N_HEADS // SCAN_HP):
        shift = (LANES - hg * SCAN_HP) % LANES
        gcol_ref[0, hg] = gates if shift == 0 else pltpu.roll(gates, shift, axis=1)

    abt = _dot_nt(wabt_ref[...], xn)
    row = lax.broadcasted_iota(jnp.int32, abt.shape, 0)
    g_row = jnp.where(row < N_HEADS, hpr_ref[0] * _softplus(abt + hpr_ref[1]), 0.0)
    gc_row = _dot3_r(g_row, ut_ref[...])
    for hg in range(N_HEADS // SCAN_HP):
        grow_ref[0, hg] = gc_row[hg * SCAN_HP:hg * SCAN_HP + 8, :]


def _chunk_tri(tm):
    t = np.arange(tm)
    same = (t[:, None] // CHUNK) == (t[None, :] // CHUNK)
    lower = same & (t[None, :] <= t[:, None])
    return lower.astype(np.float32), same.astype(np.float32)


def _gdn_in(x, pre_gain, w_in, conv_w, a_log, dt_bias, *, tm=256, col_chunk=512):
    B, L, D = x.shape
    W = N_HEADS * HEAD_DIM
    H = N_HEADS
    w_main = w_in[:, :4 * W].astype(BF16)
    wab = jnp.zeros((D, LANES), F32).at[:, :2 * H].set(w_in[:, 4 * W:4 * W + 2 * H]).astype(BF16)
    wabt = w_in[:, 4 * W:4 * W + 2 * H].T.astype(BF16)
    neg_a = -jnp.exp(a_log.astype(F32))
    hpc = jnp.zeros((2, LANES), F32).at[0, :H].set(neg_a).at[1, :H].set(dt_bias.astype(F32))
    hpr = jnp.zeros((2, 2 * H, tm), F32)
    hpr = hpr.at[0, :H, :].set(jnp.broadcast_to(neg_a[:, None], (H, tm)))
    hpr = hpr.at[1, :H, :].set(jnp.broadcast_to(dt_bias.astype(F32)[:, None], (H, tm)))
    lower, same = _chunk_tri(tm)
    lt = jnp.asarray(np.concatenate([lower, same], axis=0), BF16)
    ut = jnp.asarray(lower.T, BF16)

    kern = functools.partial(_gdn_in_kernel, tm=tm, width=W, col_chunk=col_chunk)
    row_spec = lambda n: pl.BlockSpec((1, tm, n), lambda b, i: (b, i, 0))
    head_spec = pl.BlockSpec((1, H, tm, HEAD_DIM), lambda b, i: (b, 0, i, 0))
    out_shape = (
        jax.ShapeDtypeStruct((B, H, L, HEAD_DIM), BF16),
        jax.ShapeDtypeStruct((B, H, L, HEAD_DIM), BF16),
        jax.ShapeDtypeStruct((B, H, L, HEAD_DIM), BF16),
        jax.ShapeDtypeStruct((B, L, W), BF16),
        jax.ShapeDtypeStruct((B, H // SCAN_HP, L, LANES), F32),
        jax.ShapeDtypeStruct((B, H // SCAN_HP, 8, L), F32),
    )
    return pl.pallas_call(
        kern,
        grid=(B, L // tm),
        in_specs=[
            row_spec(D),
            _resident((1, D)),
            _resident((D, 4 * W)),
            _resident((D, LANES)),
            _resident((2 * H, D)),
            _resident((CONV_K, 3 * W)),
            _resident((2, LANES)),
            _resident((2, 2 * H, tm)),
            _resident((2 * tm, tm)),
            _resident((tm, tm)),
        ],
        out_specs=(head_spec, head_spec, head_spec, row_spec(W),
                   pl.BlockSpec((1, H // SCAN_HP, tm, LANES), lambda b, i: (b, 0, i, 0)),
                   pl.BlockSpec((1, H // SCAN_HP, 8, tm), lambda b, i: (b, 0, 0, i))),
        out_shape=out_shape,
        scratch_shapes=[
            pltpu.VMEM((tm, D), BF16),
            pltpu.VMEM((tm + 8, col_chunk), F32),
            pltpu.VMEM((8, 3 * W), F32),
        ],
        compiler_params=pltpu.CompilerParams(
            dimension_semantics=("arbitrary", "arbitrary"),
            vmem_limit_bytes=VMEM_LIMIT_BYTES),
        name="gdn_in",
    )(x, pre_gain.reshape(1, D).astype(F32), w_main, wab, wabt, conv_w.astype(F32), hpc, hpr, lt, ut)


def _gdn_scan_kernel(q_ref, k_ref, v_ref, gcol_ref, grow_ref, cmask_ref, smask_ref, eye_ref,
                     o_ref, s_scr, *, hp):
    R = GROUP
    nc = R // CHUNK

    @pl.when(pl.program_id(2) == 0)
    def _():
        s_scr[...] = jnp.zeros_like(s_scr)

    gcol = gcol_ref[0, 0]
    grow = grow_ref[0, 0]

    heads = range(hp)
    gcb, glb, kbg, vb, qg, kdec, kk, qk = [], [], [], [], [], [], [], []
    for h in heads:
        qh, kh, vh = q_ref[0, h], k_ref[0, h], v_ref[0, h]
        gc_h = jnp.broadcast_to(gcol[:, h:h + 1], (R, LANES))
        beta_h = jnp.broadcast_to(gcol[:, N_HEADS + h:N_HEADS + h + 1], (R, LANES))
        gl_h = jnp.broadcast_to(gcol[:, 2 * N_HEADS + h:2 * N_HEADS + h + 1], (R, LANES))
        kf = kh.astype(F32)
        kb = kf * beta_h
        egc = jnp.exp(gc_h)
        gcb.append(gc_h)
        glb.append(gl_h)
        kbg.append((kb * egc).astype(BF16))
        vb.append((vh.astype(F32) * beta_h).astype(BF16))
        qg.append((qh.astype(F32) * egc).astype(BF16))
        kdec.append((kf * jnp.exp(gl_h - gc_h)).astype(BF16))
        kk.append(_dot_nt(kb.astype(BF16), kh))
        qk.append(_dot_nt(qh, kh))

    attn, x, p = [], [], []
    for h in heads:
        diff = jnp.concatenate([gcb[h], gcb[h]], axis=1) - grow[h:h + 1, :]
        dec = jnp.exp(jnp.where(cmask_ref[...] > 0, diff, NEG_BIG))
        attn.append((qk[h] * dec).astype(BF16))
        x.append(-(kk[h] * dec * smask_ref[...]))
        p.append(eye_ref[...] + x[h])
    for _ in range(5):
        xb = [x[h].astype(BF16) for h in heads]
        x = [_dot(xb[h], xb[h]) for h in heads]
        p = [p[h] + _dot(p[h].astype(BF16), x[h].astype(BF16)) for h in heads]
    uw = [_dot(p[h].astype(BF16), jnp.concatenate([vb[h], kbg[h]], axis=1)) for h in heads]
    u = [uw[h][:, :HEAD_DIM] for h in heads]
    wm = [uw[h][:, HEAD_DIM:].astype(BF16) for h in heads]

    s = [s_scr[h] for h in heads]
    vn_parts = [[] for _ in heads]
    for c in range(nc):
        r0 = c * CHUNK
        wq = [_dot(jnp.concatenate([wm[h][r0:r0 + CHUNK], qg[h][r0:r0 + CHUNK]], axis=0),
                   s[h].astype(BF16)) for h in heads]
        vn = [(u[h][r0:r0 + CHUNK] - wq[h][:CHUNK]).astype(BF16) for h in heads]
        pad = [jnp.zeros(((nc - 1 - c) * CHUNK, HEAD_DIM), BF16)] if c < nc - 1 else []
        for h in heads:
            vn_parts[h].append(vn[h])
            vn_all = jnp.concatenate(vn_parts[h] + pad, axis=0)
            o_c = wq[h][CHUNK:] + _dot(attn[h][r0:r0 + CHUNK, :], vn_all)
            o_ref[0, h, r0:r0 + CHUNK, :] = o_c.astype(o_ref.dtype)
        s = [s[h] * jnp.exp(glb[h][r0:r0 + 1, :]) + _dot_tn(kdec[h][r0:r0 + CHUNK], vn[h])
             for h in heads]
    for h in heads:
        s_scr[h] = s[h]


def _group_masks():
    t = np.arange(GROUP)
    same = (t[:, None] // CHUNK) == (t[None, :] // CHUNK)
    causal = same & (t[None, :] <= t[:, None])
    strict = same & (t[None, :] < t[:, None])
    return causal.astype(np.float32), strict.astype(np.float32), np.eye(GROUP, dtype=np.float32)


def _gdn_scan(q, k, v, gcol, grow):
    B, H, L, Dh = q.shape
    hp = SCAN_HP
    causal, strict, eye = _group_masks()
    head_spec = pl.BlockSpec((1, hp, GROUP, Dh), lambda b, hg, g: (b, hg, g, 0))
    return pl.pallas_call(
        functools.partial(_gdn_scan_kernel, hp=hp),
        grid=(B, H // hp, L // GROUP),
        in_specs=[
            head_spec, head_spec, head_spec,
            pl.BlockSpec((1, 1, GROUP, LANES), lambda b, hg, g: (b, hg, g, 0)),
            pl.BlockSpec((1, 1, 8, GROUP), lambda b, hg, g: (b, hg, 0, g)),
            _resident((GROUP, GROUP)), _resident((GROUP, GROUP)), _resident((GROUP, GROUP)),
        ],
        out_specs=head_spec,
        out_shape=jax.ShapeDtypeStruct((B, H, L, Dh), BF16),
        scratch_shapes=[pltpu.VMEM((hp, Dh, Dh), F32)],
        compiler_params=pltpu.CompilerParams(
            dimension_semantics=("arbitrary", "arbitrary", "arbitrary"),
            vmem_limit_bytes=VMEM_LIMIT_BYTES),
        name="gdn_scan",
    )(q, k, v, gcol, grow, jnp.asarray(causal), jnp.asarray(strict), jnp.asarray(eye))


def _head_rms(xh, gain):
    ms = jnp.mean(xh * xh, axis=-1, keepdims=True)
    return xh * lax.rsqrt(ms + EPS) * gain


def _mid_kernel(o_ref, z_ref, x_ref, onorm_ref, wo_ref, post_ref,
                kvn_ref, wkt_ref, wv_ref, wf_ref, wft_ref, fb_ref, fbr_ref, kkn_ref,
                fpre_ref, wq_ref, wz_ref, qn_ref, lt_ref, ut_ref, selq_ref, qc_ref,
                h1_ref, kt_ref, v_ref, q_ref, z2_ref,
                carry_scr, carry_r_scr, *, tm):
    @pl.when(pl.program_id(1) == 0)
    def _():
        carry_scr[...] = jnp.zeros_like(carry_scr)
        carry_r_scr[...] = jnp.zeros_like(carry_r_scr)

    gated = []
    for h in range(N_HEADS):
        on = _head_rms(o_ref[0, h].astype(F32), onorm_ref[...])
        zh = z_ref[0, :, h * HEAD_DIM:(h + 1) * HEAD_DIM].astype(F32)
        gated.append((on * (zh * _sigmoid(zh))).astype(BF16))
    y = _dot(jnp.concatenate(gated, axis=1), wo_ref[...])
    h1 = x_ref[0] + _rms(y, post_ref[...])
    h1_ref[0] = h1

    u = _rms(h1, kvn_ref[...]).astype(BF16)
    vv = _dot(u, wv_ref[...])
    for h in range(N_HEADS):
        v_ref[0, h] = vv[:, h * HEAD_DIM:(h + 1) * HEAD_DIM].astype(BF16)

    f = _dot(u, wf_ref[...]) + fb_ref[...]
    lane = lax.broadcasted_iota(jnp.int32, f.shape, 1)
    logf = jnp.where(lane < N_HEADS, -_softplus(-f), 0.0)
    c = _dot3(lt_ref[...], logf) + carry_scr[0:1, :]
    carry_scr[...] = jnp.broadcast_to(c[tm - 1:tm, :], carry_scr.shape)
    c_hi, c_mid, c_lo = _split3(c * LOG2E)
    qaug = (_dot(c_hi, selq_ref[0]) + _dot(c_mid, selq_ref[1]) + _dot(c_lo, selq_ref[2]) + qc_ref[...])

    ft = _dot_nt(wft_ref[...], u) + fbr_ref[...]
    row = lax.broadcasted_iota(jnp.int32, ft.shape, 0)
    logf_r = jnp.where(row < N_HEADS, -_softplus(-ft), 0.0)
    c_r = _dot3_r(logf_r, ut_ref[...]) + carry_r_scr[...]
    carry_r_scr[...] = jnp.broadcast_to(c_r[:, tm - 1:tm], carry_r_scr.shape)
    r_hi, r_mid, r_lo = _split3(c_r * LOG2E)
    r_hi, r_mid, r_lo = r_hi.astype(F32), r_mid.astype(F32), r_lo.astype(F32)

    kt = _dot_nt(wkt_ref[...], u)
    sub = lax.broadcasted_iota(jnp.int32, (16, tm), 0)
    for h in range(N_HEADS):
        kh = kt[h * HEAD_DIM:(h + 1) * HEAD_DIM, :]
        ms = jnp.mean(kh * kh, axis=0, keepdims=True)
        kt_ref[0, h, 0, 0:HEAD_DIM, :] = (kh * lax.rsqrt(ms + EPS) * kkn_ref[...]).astype(BF16)
        aug = jnp.where(sub < 3, 1.0,
                        jnp.where(sub == 3, -r_hi[h:h + 1],
                                  jnp.where(sub == 4, -r_mid[h:h + 1],
                                            jnp.where(sub == 5, -r_lo[h:h + 1], 0.0))))
        kt_ref[0, h, 0, HEAD_DIM:HEAD_DIM + 16, :] = aug.astype(BF16)
        kt_ref[0, h, 0, HEAD_DIM + 16:2 * HEAD_DIM, :] = jnp.zeros((HEAD_DIM - 16, tm), BF16)

    u2 = _rms(h1, fpre_ref[...]).astype(BF16)
    qq = _dot(u2, wq_ref[...])
    q_scale = HEAD_DIM ** -0.5 * LOG2E
    for h in range(N_HEADS):
        qh = _head_rms(qq[:, h * HEAD_DIM:(h + 1) * HEAD_DIM], qn_ref[...])
        q_ref[0, h, :, 0:HEAD_DIM] = (qh * q_scale).astype(BF16)
        q_ref[0, h, :, HEAD_DIM:2 * HEAD_DIM] = qaug[:, h * HEAD_DIM:(h + 1) * HEAD_DIM].astype(BF16)
    z2_ref[0] = _dot(u2, wz_ref[...]).astype(BF16)


def _aug_selectors():
    selq = np.zeros((3, LANES, N_HEADS * HEAD_DIM), np.float32)
    qc = np.zeros((1, N_HEADS * HEAD_DIM), np.float32)
    for h in range(N_HEADS):
        for t in range(3):
            selq[t, h, h * HEAD_DIM + t] = 1.0
            qc[0, h * HEAD_DIM + 3 + t] = 1.0
    return selq, qc


def _mid(o, z, x, o_norm, w_out, post_gain, kv_norm, kv_w, kv_fbias, kv_k_norm,
         fox_pre, fox_w_in, fox_q_norm, *, tm=256):
    B, L, D = x.shape
    H, W = N_HEADS, N_HEADS * HEAD_DIM
    assert tm == SLAB
    wo = w_out.astype(BF16)
    wkt = kv_w[:, :W].T.astype(BF16)
    wv = kv_w[:, W:2 * W].astype(BF16)
    wf = jnp.zeros((D, LANES), F32).at[:, :H].set(kv_w[:, 2 * W:2 * W + H]).astype(BF16)
    wft = jnp.zeros((2 * H, D), F32).at[:H, :].set(kv_w[:, 2 * W:2 * W + H].T).astype(BF16)
    fb = jnp.zeros((1, LANES), F32).at[0, :H].set(kv_fbias.astype(F32))
    fbr = jnp.zeros((2 * H, tm), F32).at[:H, :].set(
        jnp.broadcast_to(kv_fbias.astype(F32)[:, None], (H, tm)))
    kkn = jnp.broadcast_to(kv_k_norm.astype(F32)[:, None], (HEAD_DIM, tm))
    wq = fox_w_in[:, :W].astype(BF16)
    wz = fox_w_in[:, W:2 * W].astype(BF16)
    tri = np.tril(np.ones((tm, tm), np.float32))
    selq, qc = _aug_selectors()
    row = lambda n: pl.BlockSpec((1, tm, n), lambda b, i: (b, i, 0))
    head_spec = lambda n: pl.BlockSpec((1, H, tm, n), lambda b, i: (b, 0, i, 0))
    vec = lambda a, n: a.reshape(1, n).astype(F32)
    out_shape = (
        jax.ShapeDtypeStruct((B, L, D), F32),
        jax.ShapeDtypeStruct((B, H, L // tm, 2 * HEAD_DIM, tm), BF16),
        jax.ShapeDtypeStruct((B, H, L, HEAD_DIM), BF16),
        jax.ShapeDtypeStruct((B, H, L, 2 * HEAD_DIM), BF16),
        jax.ShapeDtypeStruct((B, L, W), BF16),
    )
    return pl.pallas_call(
        functools.partial(_mid_kernel, tm=tm),
        grid=(B, L // tm),
        in_specs=[
            head_spec(HEAD_DIM), row(W), row(D),
            _resident((1, HEAD_DIM)), _resident((W, D)), _resident((1, D)),
            _resident((1, D)), _resident((W, D)), _resident((D, W)), _resident((D, LANES)),
            _resident((2 * H, D)), _resident((1, LANES)), _resident((2 * H, tm)),
            _resident((HEAD_DIM, tm)),
            _resident((1, D)), _resident((D, W)), _resident((D, W)), _resident((1, HEAD_DIM)),
            _resident((tm, tm)), _resident((tm, tm)), _resident((3, LANES, W)), _resident((1, W)),
        ],
        out_specs=(row(D),
                   pl.BlockSpec((1, H, 1, 2 * HEAD_DIM, tm), lambda b, i: (b, 0, i, 0, 0)),
                   head_spec(HEAD_DIM), head_spec(2 * HEAD_DIM), row(W)),
        out_shape=out_shape,
        scratch_shapes=[pltpu.VMEM((8, LANES), F32), pltpu.VMEM((2 * H, tm), F32)],
        compiler_params=pltpu.CompilerParams(
            dimension_semantics=("arbitrary", "arbitrary"),
            vmem_limit_bytes=VMEM_LIMIT_BYTES),
        name="mid",
    )(o, z, x, vec(o_norm, HEAD_DIM), wo, vec(post_gain, D),
      vec(kv_norm, D), wkt, wv, wf, wft, fb, fbr, kkn,
      vec(fox_pre, D), wq, wz, vec(fox_q_norm, HEAD_DIM),
      jnp.asarray(tri, BF16), jnp.asarray(tri.T, BF16), jnp.asarray(selq, BF16), jnp.asarray(qc))


def _fox_attn_kernel(q_ref, kt_ref, v_ref, o_ref, acc_scr, m_scr, *, tq, hp):
    ts = SLAB
    i = pl.program_id(2)
    n_diag = tq // ts
    lane = lax.broadcasted_iota(jnp.int32, (ts, HEAD_DIM), 1)
    ones_col = jnp.where(lane == 0, 1.0, 0.0).astype(BF16)
    col = lax.broadcasted_iota(jnp.int32, (2 * ts, ts), 1)
    rowi = lax.broadcasted_iota(jnp.int32, (2 * ts, ts), 0)
    pair_masks = (col <= rowi, col + ts <= rowi)

    def v_aug(h, j):
        r0 = pl.multiple_of(j * ts, ts)
        return jnp.concatenate([v_ref[0, h, pl.ds(r0, ts), :], ones_col], axis=1)

    def scores(h, slabs, row0):
        q = q_ref[0, h, row0:tq, :]
        return [_dot(q, kt_ref[0, h, j]) for j in slabs]

    def finish(h, parts, slabs, row0, diag):
        if diag:
            parts = [jnp.concatenate([jnp.where(mk, p[:2 * ts], NEG_BIG), p[2 * ts:]], axis=0)
                     if p.shape[0] > 2 * ts else jnp.where(mk, p, NEG_BIG)
                     for p, mk in zip(parts, pair_masks)]
        mx = parts[0]
        for p in parts[1:]:
            mx = jnp.maximum(mx, p)
        m_old = m_scr[h, row0:tq, :]
        m_new = jnp.maximum(m_old, jnp.max(mx, axis=1, keepdims=True))
        m_scr[h, row0:tq, :] = m_new
        alpha = jnp.exp2(m_old - m_new)
        m_wide = jnp.concatenate([m_new, m_new], axis=1)
        pv = None
        for p, j in zip(parts, slabs):
            d = _dot(jnp.exp2(p - m_wide).astype(BF16), v_aug(h, j))
            pv = d if pv is None else pv + d
        acc_scr[h, row0:tq, :] = jnp.concatenate([alpha, alpha], axis=1) * acc_scr[h, row0:tq, :] + pv

    acc_scr[...] = jnp.zeros_like(acc_scr)
    m_scr[...] = jnp.full(m_scr.shape, NEG_BIG, F32)

    def step(slabs, row0, diag):
        parts = [scores(h, slabs, row0) for h in range(hp)]
        for h in range(hp):
            finish(h, parts[h], slabs, row0, diag)

    def full_blocks(jj, carry):
        step((2 * jj, 2 * jj + 1), 0, False)
        return carry

    lax.fori_loop(0, i * (n_diag // 2), full_blocks, 0)
    for d in range(n_diag // 2):
        step((i * n_diag + 2 * d, i * n_diag + 2 * d + 1), 2 * d * ts, True)
    for h in range(hp):
        acc = acc_scr[h]
        o_ref[0, h] = (acc[:, :HEAD_DIM] / acc[:, HEAD_DIM:HEAD_DIM + 1]).astype(o_ref.dtype)


def _fox_attn(q2, kt2, v, *, tq=1024, hp=4):
    B, H, L, Dh = v.shape
    tq = min(tq, L)
    assert tq % (2 * SLAB) == 0 and L % tq == 0 and H % hp == 0
    return pl.pallas_call(
        functools.partial(_fox_attn_kernel, tq=tq, hp=hp),
        grid=(B, H // hp, L // tq),
        in_specs=[pl.BlockSpec((1, hp, tq, 2 * Dh), lambda b, h, i: (b, h, i, 0)),
                  pl.BlockSpec((1, hp, L // SLAB, 2 * Dh, SLAB), lambda b, h, i: (b, h, 0, 0, 0)),
                  pl.BlockSpec((1, hp, L, Dh), lambda b, h, i: (b, h, 0, 0))],
        out_specs=pl.BlockSpec((1, hp, tq, Dh), lambda b, h, i: (b, h, i, 0)),
        out_shape=jax.ShapeDtypeStruct((B, H, L, Dh), BF16),
        scratch_shapes=[pltpu.VMEM((hp, tq, 2 * Dh), F32), pltpu.VMEM((hp, tq, LANES), F32)],
        compiler_params=pltpu.CompilerParams(
            dimension_semantics=("arbitrary", "arbitrary", "arbitrary"),
            vmem_limit_bytes=VMEM_LIMIT_BYTES),
        name="fox_attn",
    )(q2, kt2, v)


def _fox_out_kernel(o_ref, z_ref, h_ref, wo_ref, post_ref, out_ref):
    gated = []
    for h in range(N_HEADS):
        zh = z_ref[0, :, h * HEAD_DIM:(h + 1) * HEAD_DIM].astype(F32)
        gated.append((o_ref[0, h].astype(F32) * (zh * _sigmoid(zh))).astype(BF16))
    y = _dot(jnp.concatenate(gated, axis=1), wo_ref[...])
    out_ref[0] = h_ref[0] + _rms(y, post_ref[...])


def _fox_out(o, z2, h1, w_out, post_gain, *, tm=256):
    B, L, D = h1.shape
    H, W = N_HEADS, N_HEADS * HEAD_DIM
    row = lambda n: pl.BlockSpec((1, tm, n), lambda b, i: (b, i, 0))
    return pl.pallas_call(
        _fox_out_kernel,
        grid=(B, L // tm),
        in_specs=[pl.BlockSpec((1, H, tm, HEAD_DIM), lambda b, i: (b, 0, i, 0)), row(W), row(D),
                  _resident((W, D)), _resident((1, D))],
        out_specs=row(D),
        out_shape=jax.ShapeDtypeStruct((B, L, D), F32),
        compiler_params=pltpu.CompilerParams(
            dimension_semantics=("arbitrary", "arbitrary"),
            vmem_limit_bytes=VMEM_LIMIT_BYTES),
        name="fox_out",
    )(o, z2, h1, w_out.astype(BF16), post_gain.reshape(1, D).astype(F32))


def kernel(x, gdn_pre_norm, gdn_w_in, gdn_conv_w, gdn_a_log, gdn_dt_bias, gdn_o_norm, gdn_w_out, gdn_post_norm, kv_norm, kv_w, kv_forget_bias, kv_k_norm, fox_pre_norm, fox_w_in, fox_q_norm, fox_w_out, fox_post_norm):
    assert gdn_w_in.shape[0] == 1 and fox_w_in.shape[0] == 1, "one GDN layer then one FoX layer"
    q, k, v, z, gcol, grow = _gdn_in(x, gdn_pre_norm[0], gdn_w_in[0], gdn_conv_w[0],
                                     gdn_a_log[0], gdn_dt_bias[0])
    o = _gdn_scan(q, k, v, gcol, grow)
    h1, kt2, vs, q2, z2 = _mid(
        o, z, x, gdn_o_norm[0], gdn_w_out[0], gdn_post_norm[0],
        kv_norm, kv_w, kv_forget_bias, kv_k_norm, fox_pre_norm[0], fox_w_in[0], fox_q_norm[0])
    o2 = _fox_attn(q2, kt2, vs)
    return _fox_out(o2, z2, h1, fox_w_out[0], fox_post_norm[0])
```

```python
import functools
import math

import jax
import jax.numpy as jnp
import numpy as np
from jax import lax
from jax.experimental import pallas as pl
from jax.experimental.pallas import tpu as pltpu

HEAD_DIM = 128
N_HEADS = 8
CONV_K = 4
CHUNK = 64
GROUP = 256
PAIR = 128
SCAN_HP = 8
SLAB = 256
ROW_TILE = 512
LOG2E = math.log2(math.e)
EPS = 1e-6
LANES = 128
VMEM_LIMIT_BYTES = 56 * 1024 * 1024
NEG_BIG = -1e30

F32 = jnp.float32
BF16 = jnp.bfloat16


def _dot(a, b):
    return jnp.dot(a, b, preferred_element_type=F32)


def _dot_nt(a, b):
    return lax.dot_general(a, b, (((1,), (1,)), ((), ())), preferred_element_type=F32)


def _dot_tn(a, b):
    return lax.dot_general(a, b, (((0,), (0,)), ((), ())), preferred_element_type=F32)


def _split3(x):
    hi = x.astype(BF16)
    r1 = x - hi.astype(F32)
    mid = r1.astype(BF16)
    lo = (r1 - mid.astype(F32)).astype(BF16)
    return hi, mid, lo


def _dot3(a_bf16, x):
    hi, mid, lo = _split3(x)
    return _dot(a_bf16, hi) + _dot(a_bf16, mid) + _dot(a_bf16, lo)


def _dot3_r(x, b_bf16):
    hi, mid, lo = _split3(x)
    return _dot(hi, b_bf16) + _dot(mid, b_bf16) + _dot(lo, b_bf16)


def _sigmoid(x):
    return 1.0 / (1.0 + jnp.exp(-x))


def _softplus(x):
    return jnp.maximum(x, 0.0) + jnp.log(1.0 + jnp.exp(-jnp.abs(x)))


def _rms(x, gain):
    ms = jnp.mean(x * x, axis=-1, keepdims=True)
    return x * lax.rsqrt(ms + EPS) * gain


def _resident(shape):
    nd = len(shape)
    return pl.BlockSpec(shape, lambda *_: (0,) * nd, pipeline_mode=pl.Buffered(1))


def _gdn_in_kernel(x_ref, gain_ref, w_ref, wab_ref, wabt_ref, convw_ref, hpc_ref, hpr_ref,
                   lt_ref, ut_ref,
                   q_ref, k_ref, v_ref, z_ref, gcol_ref, grow_ref,
                   xn_scr, pre_scr, carry_scr, *, tm, width, col_chunk):
    @pl.when(pl.program_id(1) == 0)
    def _():
        carry_scr[...] = jnp.zeros_like(carry_scr)

    xn_scr[...] = _rms(x_ref[0], gain_ref[...]).astype(BF16)
    xn = xn_scr[...]
    q_scale = HEAD_DIM ** -0.5

    n_qkv = 3 * width // col_chunk
    for c in range(n_qkv):
        lo = c * col_chunk
        pc = _dot(xn, w_ref[:, lo:lo + col_chunk])
        pre_scr[0:8, :] = carry_scr[:, lo:lo + col_chunk]
        pre_scr[8:8 + tm, :] = pc
        carry_scr[:, lo:lo + col_chunk] = pc[tm - 8:tm, :]
        cw = convw_ref[:, lo:lo + col_chunk]
        acc = pc * cw[CONV_K - 1:CONV_K, :]
        for j in range(1, CONV_K):
            acc = acc + pre_scr[8 - j:8 - j + tm, :] * cw[CONV_K - 1 - j:CONV_K - j, :]
        y = acc * _sigmoid(acc)
        for hh in range(col_chunk // HEAD_DIM):
            col = lo + hh * HEAD_DIM
            yh = y[:, hh * HEAD_DIM:(hh + 1) * HEAD_DIM]
            if col < 2 * width:
                ss = jnp.sum(yh * yh, axis=-1, keepdims=True)
                yh = yh * lax.rsqrt(ss + EPS)
            head = (col % width) // HEAD_DIM
            if col < width:
                q_ref[0, head] = (yh * q_scale).astype(BF16)
            elif col < 2 * width:
                k_ref[0, head] = yh.astype(BF16)
            else:
                v_ref[0, head] = yh.astype(BF16)
    for c in range(width // col_chunk):
        lo = 3 * width + c * col_chunk
        z_ref[0, :, c * col_chunk:(c + 1) * col_chunk] = _dot(xn, w_ref[:, lo:lo + col_chunk]).astype(BF16)

    ab = _dot(xn, wab_ref[...])
    lane = lax.broadcasted_iota(jnp.int32, ab.shape, 1)
    g_col = jnp.where(lane < N_HEADS, hpc_ref[0:1, :] * _softplus(ab + hpc_ref[1:2, :]), 0.0)
    beta_col = _sigmoid(ab)
    cums = _dot3(lt_ref[...], g_col)
    gc_col = cums[0:tm]
    gl_col = pltpu.roll(cums[tm:2 * tm], 2 * N_HEADS, axis=1)
    gates = jnp.where(lane < N_HEADS, gc_col, jnp.where(lane < 2 * N_HEADS, beta_col, gl_col))
    for hg in range(N_HEADS // SCAN_HP):
        shift = (LANES - hg * SCAN_HP) % LANES
        gcol_ref[0, hg] = gates if shift == 0 else pltpu.roll(gates, shift, axis=1)

    abt = _dot_nt(wabt_ref[...], xn)
    row = lax.broadcasted_iota(jnp.int32, abt.shape, 0)
    g_row = jnp.where(row < N_HEADS, hpr_ref[0] * _softplus(abt + hpr_ref[1]), 0.0)
    gc_row = _dot3_r(g_row, ut_ref[...])
    for hg in range(N_HEADS // SCAN_HP):
        grow_ref[0, hg] = gc_row[hg * SCAN_HP:hg * SCAN_HP + 8, :]


def _chunk_tri(tm):
    t = np.arange(tm)
    same = (t[:, None] // CHUNK) == (t[None, :] // CHUNK)
    lower = same & (t[None, :] <= t[:, None])
    return lower.astype(np.float32), same.astype(np.float32)


def _gdn_in(x, pre_gain, w_in, conv_w, a_log, dt_bias, *, tm=GROUP, col_chunk=512):
    B, L, D = x.shape
    W = N_HEADS * HEAD_DIM
    H = N_HEADS
    w_main = w_in[:, :4 * W].astype(BF16)
    wab = jnp.zeros((D, LANES), F32).at[:, :2 * H].set(w_in[:, 4 * W:4 * W + 2 * H]).astype(BF16)
    wabt = w_in[:, 4 * W:4 * W + 2 * H].T.astype(BF16)
    neg_a = -jnp.exp(a_log.astype(F32))
    hpc = jnp.zeros((2, LANES), F32).at[0, :H].set(neg_a).at[1, :H].set(dt_bias.astype(F32))
    hpr = jnp.zeros((2, 2 * H, tm), F32)
    hpr = hpr.at[0, :H, :].set(jnp.broadcast_to(neg_a[:, None], (H, tm)))
    hpr = hpr.at[1, :H, :].set(jnp.broadcast_to(dt_bias.astype(F32)[:, None], (H, tm)))
    lower, same = _chunk_tri(tm)
    lt = jnp.asarray(np.concatenate([lower, same], axis=0), BF16)
    ut = jnp.asarray(lower.T, BF16)

    kern = functools.partial(_gdn_in_kernel, tm=tm, width=W, col_chunk=col_chunk)
    row_spec = lambda n: pl.BlockSpec((1, tm, n), lambda b, i: (b, i, 0))
    head_spec = pl.BlockSpec((1, H, tm, HEAD_DIM), lambda b, i: (b, 0, i, 0))
    out_shape = (
        jax.ShapeDtypeStruct((B, H, L, HEAD_DIM), BF16),
        jax.ShapeDtypeStruct((B, H, L, HEAD_DIM), BF16),
        jax.ShapeDtypeStruct((B, H, L, HEAD_DIM), BF16),
        jax.ShapeDtypeStruct((B, L, W), BF16),
        jax.ShapeDtypeStruct((B, H // SCAN_HP, L, LANES), F32),
        jax.ShapeDtypeStruct((B, H // SCAN_HP, 8, L), F32),
    )
    return pl.pallas_call(
        kern,
        grid=(B, L // tm),
        in_specs=[
            row_spec(D),
            _resident((1, D)),
            _resident((D, 4 * W)),
            _resident((D, LANES)),
            _resident((2 * H, D)),
            _resident((CONV_K, 3 * W)),
            _resident((2, LANES)),
            _resident((2, 2 * H, tm)),
            _resident((2 * tm, tm)),
            _resident((tm, tm)),
        ],
        out_specs=(head_spec, head_spec, head_spec, row_spec(W),
                   pl.BlockSpec((1, H // SCAN_HP, tm, LANES), lambda b, i: (b, 0, i, 0)),
                   pl.BlockSpec((1, H // SCAN_HP, 8, tm), lambda b, i: (b, 0, 0, i))),
        out_shape=out_shape,
        scratch_shapes=[
            pltpu.VMEM((tm, D), BF16),
            pltpu.VMEM((tm + 8, col_chunk), F32),
            pltpu.VMEM((8, 3 * W), F32),
        ],
        compiler_params=pltpu.CompilerParams(
            dimension_semantics=("arbitrary", "arbitrary"),
            vmem_limit_bytes=VMEM_LIMIT_BYTES),
        name="gdn_in",
    )(x, pre_gain.reshape(1, D).astype(F32), w_main, wab, wabt, conv_w.astype(F32), hpc, hpr, lt, ut)


def _gdn_scan_kernel(q_ref, k_ref, v_ref, gcol_ref, grow_ref, cmask_ref, smask_ref, eye_ref,
                     o_ref, s_scr, *, hp):
    P = PAIR
    n_pair = GROUP // P
    cpp = P // CHUNK

    @pl.when(pl.program_id(2) == 0)
    def _():
        s_scr[...] = jnp.zeros_like(s_scr)

    gcol = gcol_ref[0, 0]
    grow = grow_ref[0, 0]
    cmask = cmask_ref[...] > 0
    smask = smask_ref[...]
    eye = eye_ref[...]

    probs = [(h, j) for h in range(hp) for j in range(n_pair)]
    idx = {hj: n for n, hj in enumerate(probs)}
    glb, kbg, vb, qg, kdec, attn, x, p = [], [], [], [], [], [], [], []
    for h, j in probs:
        rows = slice(j * P, (j + 1) * P)
        qh, kh, vh = q_ref[0, h, rows, :], k_ref[0, h, rows, :], v_ref[0, h, rows, :]
        gc = jnp.broadcast_to(gcol[rows, h:h + 1], (P, LANES))
        beta = jnp.broadcast_to(gcol[rows, N_HEADS + h:N_HEADS + h + 1], (P, LANES))
        gl = jnp.broadcast_to(gcol[rows, 2 * N_HEADS + h:2 * N_HEADS + h + 1], (P, LANES))
        kf = kh.astype(F32)
        kb = kf * beta
        egc = jnp.exp(gc)
        glb.append(gl)
        kbg.append((kb * egc).astype(BF16))
        vb.append((vh.astype(F32) * beta).astype(BF16))
        qg.append((qh.astype(F32) * egc).astype(BF16))
        kdec.append((kf * jnp.exp(gl - gc)).astype(BF16))
        kk = _dot_nt(kb.astype(BF16), kh)
        qk = _dot_nt(qh, kh)
        diff = gc - grow[h:h + 1, rows]
        dec = jnp.exp(jnp.where(cmask, diff, NEG_BIG))
        attn.append((qk * dec).astype(BF16))
        x.append(-(kk * dec * smask))
        p.append(eye + x[-1])
    n = range(len(probs))
    xb = [x[i].astype(BF16) for i in n]
    x = [_dot(xb[i], xb[i]) for i in n]
    for _ in range(4):
        xb = [x[i].astype(BF16) for i in n]
        r = [_dot(jnp.concatenate([xb[i], p[i].astype(BF16)], axis=0), xb[i]) for i in n]
        x = [r[i][:P] for i in n]
        p = [p[i] + r[i][P:] for i in n]
    p = [p[i] + _dot(p[i].astype(BF16), x[i].astype(BF16)) for i in n]
    uw = [_dot(p[i].astype(BF16), jnp.concatenate([vb[i], kbg[i]], axis=1)) for i in n]
    u = [uw[i][:, :HEAD_DIM] for i in n]
    wm = [uw[i][:, HEAD_DIM:].astype(BF16) for i in n]

    heads = range(hp)
    s = [s_scr[h] for h in heads]
    for j in range(n_pair):
        vn_parts = [[] for _ in heads]
        for c in range(cpp):
            r0 = c * CHUNK
            pi = [idx[(h, j)] for h in heads]
            wq = [_dot(jnp.concatenate([wm[pi[h]][r0:r0 + CHUNK], qg[pi[h]][r0:r0 + CHUNK]], axis=0),
                       s[h].astype(BF16)) for h in heads]
            vn = [(u[pi[h]][r0:r0 + CHUNK] - wq[h][:CHUNK]).astype(BF16) for h in heads]
            pad = [jnp.zeros(((cpp - 1 - c) * CHUNK, HEAD_DIM), BF16)] if c < cpp - 1 else []
            for h in heads:
                vn_parts[h].append(vn[h])
                vn_all = jnp.concatenate(vn_parts[h] + pad, axis=0)
                o_c = wq[h][CHUNK:] + _dot(attn[pi[h]][r0:r0 + CHUNK, :], vn_all)
                o_ref[0, h, j * P + r0:j * P + r0 + CHUNK, :] = o_c.astype(o_ref.dtype)
            s = [s[h] * jnp.exp(glb[pi[h]][r0:r0 + 1, :]) + _dot_tn(kdec[pi[h]][r0:r0 + CHUNK], vn[h])
                 for h in heads]
    for h in heads:
        s_scr[h] = s[h]


def _pair_masks():
    t = np.arange(PAIR)
    same = (t[:, None] // CHUNK) == (t[None, :] // CHUNK)
    causal = same & (t[None, :] <= t[:, None])
    strict = same & (t[None, :] < t[:, None])
    return causal.astype(np.float32), strict.astype(np.float32), np.eye(PAIR, dtype=np.float32)


def _gdn_scan(q, k, v, gcol, grow):
    B, H, L, Dh = q.shape
    hp = SCAN_HP
    causal, strict, eye = _pair_masks()
    head_spec = pl.BlockSpec((1, hp, GROUP, Dh), lambda b, hg, g: (b, hg, g, 0))
    return pl.pallas_call(
        functools.partial(_gdn_scan_kernel, hp=hp),
        grid=(B, H // hp, L // GROUP),
        in_specs=[
            head_spec, head_spec, head_spec,
            pl.BlockSpec((1, 1, GROUP, LANES), lambda b, hg, g: (b, hg, g, 0)),
            pl.BlockSpec((1, 1, 8, GROUP), lambda b, hg, g: (b, hg, 0, g)),
            _resident((PAIR, PAIR)), _resident((PAIR, PAIR)), _resident((PAIR, PAIR)),
        ],
        out_specs=head_spec,
        out_shape=jax.ShapeDtypeStruct((B, H, L, Dh), BF16),
        scratch_shapes=[pltpu.VMEM((hp, Dh, Dh), F32)],
        compiler_params=pltpu.CompilerParams(
            dimension_semantics=("arbitrary", "arbitrary", "arbitrary"),
            vmem_limit_bytes=VMEM_LIMIT_BYTES),
        name="gdn_scan",
    )(q, k, v, gcol, grow, jnp.asarray(causal), jnp.asarray(strict), jnp.asarray(eye))


def _head_rms(xh, gain):
    ms = jnp.mean(xh * xh, axis=-1, keepdims=True)
    return xh * lax.rsqrt(ms + EPS) * gain


def _mid_kernel(o_ref, z_ref, x_ref, onorm_ref, wo_ref, post_ref,
                kvn_ref, wkt_ref, wv_ref, wf_ref, wft_ref, fb_ref, fbr_ref, kkn_ref,
                fpre_ref, wq_ref, wz_ref, qn_ref, lt_ref, ut_ref, selq_ref, qc_ref,
                h1_ref, kt_ref, v_ref, q_ref, z2_ref,
                carry_scr, carry_r_scr, *, tm):
    @pl.when(pl.program_id(1) == 0)
    def _():
        carry_scr[...] = jnp.zeros_like(carry_scr)
        carry_r_scr[...] = jnp.zeros_like(carry_r_scr)

    gated = []
    for h in range(N_HEADS):
        on = _head_rms(o_ref[0, h].astype(F32), onorm_ref[...])
        zh = z_ref[0, :, h * HEAD_DIM:(h + 1) * HEAD_DIM].astype(F32)
        gated.append((on * (zh * _sigmoid(zh))).astype(BF16))
    y = _dot(jnp.concatenate(gated, axis=1), wo_ref[...])
    h1 = x_ref[0] + _rms(y, post_ref[...])
    h1_ref[0] = h1

    u = _rms(h1, kvn_ref[...]).astype(BF16)
    vv = _dot(u, wv_ref[...])
    for h in range(N_HEADS):
        v_ref[0, h] = vv[:, h * HEAD_DIM:(h + 1) * HEAD_DIM].astype(BF16)

    f = _dot(u, wf_ref[...]) + fb_ref[...]
    lane = lax.broadcasted_iota(jnp.int32, f.shape, 1)
    logf = jnp.where(lane < N_HEADS, -_softplus(-f), 0.0)
    c = _dot3(lt_ref[...], logf) + carry_scr[0:1, :]
    carry_scr[...] = jnp.broadcast_to(c[tm - 1:tm, :], carry_scr.shape)
    c_hi, c_mid, c_lo = _split3(c * LOG2E)
    qaug = (_dot(c_hi, selq_ref[0]) + _dot(c_mid, selq_ref[1]) + _dot(c_lo, selq_ref[2]) + qc_ref[...])

    ft = _dot_nt(wft_ref[...], u) + fbr_ref[...]
    row = lax.broadcasted_iota(jnp.int32, ft.shape, 0)
    logf_r = jnp.where(row < N_HEADS, -_softplus(-ft), 0.0)
    c_r = _dot3_r(logf_r, ut_ref[...]) + carry_r_scr[...]
    carry_r_scr[...] = jnp.broadcast_to(c_r[:, tm - 1:tm], carry_r_scr.shape)
    r_hi, r_mid, r_lo = _split3(c_r * LOG2E)
    r_hi, r_mid, r_lo = r_hi.astype(F32), r_mid.astype(F32), r_lo.astype(F32)

    kt = _dot_nt(wkt_ref[...], u)
    sub = lax.broadcasted_iota(jnp.int32, (16, tm), 0)
    for h in range(N_HEADS):
        kh = kt[h * HEAD_DIM:(h + 1) * HEAD_DIM, :]
        ms = jnp.mean(kh * kh, axis=0, keepdims=True)
        khn = (kh * lax.rsqrt(ms + EPS) * kkn_ref[...]).astype(BF16)
        aug = jnp.where(sub < 3, 1.0,
                        jnp.where(sub == 3, -r_hi[h:h + 1],
                                  jnp.where(sub == 4, -r_mid[h:h + 1],
                                            jnp.where(sub == 5, -r_lo[h:h + 1], 0.0)))).astype(BF16)
        for sl in range(tm // SLAB):
            cols = slice(sl * SLAB, (sl + 1) * SLAB)
            kt_ref[0, h, sl, 0:HEAD_DIM, :] = khn[:, cols]
            kt_ref[0, h, sl, HEAD_DIM:HEAD_DIM + 16, :] = aug[:, cols]
            kt_ref[0, h, sl, HEAD_DIM + 16:2 * HEAD_DIM, :] = jnp.zeros((HEAD_DIM - 16, SLAB), BF16)

    u2 = _rms(h1, fpre_ref[...]).astype(BF16)
    qq = _dot(u2, wq_ref[...])
    q_scale = HEAD_DIM ** -0.5 * LOG2E
    for h in range(N_HEADS):
        qh = _head_rms(qq[:, h * HEAD_DIM:(h + 1) * HEAD_DIM], qn_ref[...])
        q_ref[0, h, :, 0:HEAD_DIM] = (qh * q_scale).astype(BF16)
        q_ref[0, h, :, HEAD_DIM:2 * HEAD_DIM] = qaug[:, h * HEAD_DIM:(h + 1) * HEAD_DIM].astype(BF16)
    z2_ref[0] = _dot(u2, wz_ref[...]).astype(BF16)


def _aug_selectors():
    selq = np.zeros((3, LANES, N_HEADS * HEAD_DIM), np.float32)
    qc = np.zeros((1, N_HEADS * HEAD_DIM), np.float32)
    for h in range(N_HEADS):
        for t in range(3):
            selq[t, h, h * HEAD_DIM + t] = 1.0
            qc[0, h * HEAD_DIM + 3 + t] = 1.0
    return selq, qc


def _mid(o, z, x, o_norm, w_out, post_gain, kv_norm, kv_w, kv_fbias, kv_k_norm,
         fox_pre, fox_w_in, fox_q_norm, *, tm=ROW_TILE):
    B, L, D = x.shape
    H, W = N_HEADS, N_HEADS * HEAD_DIM
    assert tm % SLAB == 0
    wo = w_out.astype(BF16)
    wkt = kv_w[:, :W].T.astype(BF16)
    wv = kv_w[:, W:2 * W].astype(BF16)
    wf = jnp.zeros((D, LANES), F32).at[:, :H].set(kv_w[:, 2 * W:2 * W + H]).astype(BF16)
    wft = jnp.zeros((2 * H, D), F32).at[:H, :].set(kv_w[:, 2 * W:2 * W + H].T).astype(BF16)
    fb = jnp.zeros((1, LANES), F32).at[0, :H].set(kv_fbias.astype(F32))
    fbr = jnp.zeros((2 * H, tm), F32).at[:H, :].set(
        jnp.broadcast_to(kv_fbias.astype(F32)[:, None], (H, tm)))
    kkn = jnp.broadcast_to(kv_k_norm.astype(F32)[:, None], (HEAD_DIM, tm))
    wq = fox_w_in[:, :W].astype(BF16)
    wz = fox_w_in[:, W:2 * W].astype(BF16)
    tri = np.tril(np.ones((tm, tm), np.float32))
    selq, qc = _aug_selectors()
    row = lambda n: pl.BlockSpec((1, tm, n), lambda b, i: (b, i, 0))
    head_spec = lambda n: pl.BlockSpec((1, H, tm, n), lambda b, i: (b, 0, i, 0))
    vec = lambda a, n: a.reshape(1, n).astype(F32)
    out_shape = (
        jax.ShapeDtypeStruct((B, L, D), F32),
        jax.ShapeDtypeStruct((B, H, L // SLAB, 2 * HEAD_DIM, SLAB), BF16),
        jax.ShapeDtypeStruct((B, H, L, HEAD_DIM), BF16),
        jax.ShapeDtypeStruct((B, H, L, 2 * HEAD_DIM), BF16),
        jax.ShapeDtypeStruct((B, L, W), BF16),
    )
    return pl.pallas_call(
        functools.partial(_mid_kernel, tm=tm),
        grid=(B, L // tm),
        in_specs=[
            head_spec(HEAD_DIM), row(W), row(D),
            _resident((1, HEAD_DIM)), _resident((W, D)), _resident((1, D)),
            _resident((1, D)), _resident((W, D)), _resident((D, W)), _resident((D, LANES)),
            _resident((2 * H, D)), _resident((1, LANES)), _resident((2 * H, tm)),
            _resident((HEAD_DIM, tm)),
            _resident((1, D)), _resident((D, W)), _resident((D, W)), _resident((1, HEAD_DIM)),
            _resident((tm, tm)), _resident((tm, tm)), _resident((3, LANES, W)), _resident((1, W)),
        ],
        out_specs=(row(D),
                   pl.BlockSpec((1, H, tm // SLAB, 2 * HEAD_DIM, SLAB), lambda b, i: (b, 0, i, 0, 0)),
                   head_spec(HEAD_DIM), head_spec(2 * HEAD_DIM), row(W)),
        out_shape=out_shape,
        scratch_shapes=[pltpu.VMEM((8, LANES), F32), pltpu.VMEM((2 * H, tm), F32)],
        compiler_params=pltpu.CompilerParams(
            dimension_semantics=("arbitrary", "arbitrary"),
            vmem_limit_bytes=VMEM_LIMIT_BYTES),
        name="mid",
    )(o, z, x, vec(o_norm, HEAD_DIM), wo, vec(post_gain, D),
      vec(kv_norm, D), wkt, wv, wf, wft, fb, fbr, kkn,
      vec(fox_pre, D), wq, wz, vec(fox_q_norm, HEAD_DIM),
      jnp.asarray(tri, BF16), jnp.asarray(tri.T, BF16), jnp.asarray(selq, BF16), jnp.asarray(qc))


def _fox_attn_kernel(q_ref, kt_ref, v_ref, o_ref, acc_scr, m_scr, *, tq, hp):
    ts = SLAB
    i = pl.program_id(2)
    n_diag = tq // ts
    lane = lax.broadcasted_iota(jnp.int32, (ts, HEAD_DIM), 1)
    ones_col = jnp.where(lane == 0, 1.0, 0.0).astype(BF16)
    col = lax.broadcasted_iota(jnp.int32, (2 * ts, ts), 1)
    rowi = lax.broadcasted_iota(jnp.int32, (2 * ts, ts), 0)
    pair_masks = (col <= rowi, col + ts <= rowi)

    def v_aug(h, j):
        r0 = pl.multiple_of(j * ts, ts)
        return jnp.concatenate([v_ref[0, h, pl.ds(r0, ts), :], ones_col], axis=1)

    def scores(h, slabs, row0):
        q = q_ref[0, h, row0:tq, :]
        return [_dot(q, kt_ref[0, h, j]) for j in slabs]

    def finish(h, parts, slabs, row0, diag):
        if diag:
            parts = [jnp.concatenate([jnp.where(mk, p[:2 * ts], NEG_BIG), p[2 * ts:]], axis=0)
                     if p.shape[0] > 2 * ts else jnp.where(mk, p, NEG_BIG)
                     for p, mk in zip(parts, pair_masks)]
        mx = parts[0]
        for p in parts[1:]:
            mx = jnp.maximum(mx, p)
        m_old = m_scr[h, row0:tq, :]
        m_new = jnp.maximum(m_old, jnp.max(mx, axis=1, keepdims=True))
        m_scr[h, row0:tq, :] = m_new
        alpha = jnp.exp2(m_old - m_new)
        m_wide = jnp.concatenate([m_new, m_new], axis=1)
        pv = None
        for p, j in zip(parts, slabs):
            d = _dot(jnp.exp2(p - m_wide).astype(BF16), v_aug(h, j))
            pv = d if pv is None else pv + d
        acc_scr[h, row0:tq, :] = jnp.concatenate([alpha, alpha], axis=1) * acc_scr[h, row0:tq, :] + pv

    acc_scr[...] = jnp.zeros_like(acc_scr)
    m_scr[...] = jnp.full(m_scr.shape, NEG_BIG, F32)

    def step(slabs, row0, diag):
        parts = [scores(h, slabs, row0) for h in range(hp)]
        for h in range(hp):
            finish(h, parts[h], slabs, row0, diag)

    def full_blocks(jj, carry):
        step((2 * jj, 2 * jj + 1), 0, False)
        return carry

    lax.fori_loop(0, i * (n_diag // 2), full_blocks, 0)
    for d in range(n_diag // 2):
        step((i * n_diag + 2 * d, i * n_diag + 2 * d + 1), 2 * d * ts, True)
    for h in range(hp):
        acc = acc_scr[h]
        o_ref[0, h] = (acc[:, :HEAD_DIM] / acc[:, HEAD_DIM:HEAD_DIM + 1]).astype(o_ref.dtype)


def _fox_attn(q2, kt2, v, *, tq=1024, hp=4):
    B, H, L, Dh = v.shape
    tq = min(tq, L)
    assert tq % (2 * SLAB) == 0 and L % tq == 0 and H % hp == 0
    return pl.pallas_call(
        functools.partial(_fox_attn_kernel, tq=tq, hp=hp),
        grid=(B, H // hp, L // tq),
        in_specs=[pl.BlockSpec((1, hp, tq, 2 * Dh), lambda b, h, i: (b, h, i, 0)),
                  pl.BlockSpec((1, hp, L // SLAB, 2 * Dh, SLAB), lambda b, h, i: (b, h, 0, 0, 0)),
                  pl.BlockSpec((1, hp, L, Dh), lambda b, h, i: (b, h, 0, 0))],
        out_specs=pl.BlockSpec((1, hp, tq, Dh), lambda b, h, i: (b, h, i, 0)),
        out_shape=jax.ShapeDtypeStruct((B, H, L, Dh), BF16),
        scratch_shapes=[pltpu.VMEM((hp, tq, 2 * Dh), F32), pltpu.VMEM((hp, tq, LANES), F32)],
        compiler_params=pltpu.CompilerParams(
            dimension_semantics=("arbitrary", "arbitrary", "arbitrary"),
            vmem_limit_bytes=VMEM_LIMIT_BYTES),
        name="fox_attn",
    )(q2, kt2, v)


def _fox_out_kernel(o_ref, z_ref, h_ref, wo_ref, post_ref, out_ref):
    gated = []
    for h in range(N_HEADS):
        zh = z_ref[0, :, h * HEAD_DIM:(h + 1) * HEAD_DIM].astype(F32)
        gated.append((o_ref[0, h].astype(F32) * (zh * _sigmoid(zh))).astype(BF16))
    y = _dot(jnp.concatenate(gated, axis=1), wo_ref[...])
    out_ref[0] = h_ref[0] + _rms(y, post_ref[...])


def _fox_out(o, z2, h1, w_out, post_gain, *, tm=ROW_TILE):
    B, L, D = h1.shape
    H, W = N_HEADS, N_HEADS * HEAD_DIM
    row = lambda n: pl.BlockSpec((1, tm, n), lambda b, i: (b, i, 0))
    return pl.pallas_call(
        _fox_out_kernel,
        grid=(B, L // tm),
        in_specs=[pl.BlockSpec((1, H, tm, HEAD_DIM), lambda b, i: (b, 0, i, 0)), row(W), row(D),
                  _resident((W, D)), _resident((1, D))],
        out_specs=row(D),
        out_shape=jax.ShapeDtypeStruct((B, L, D), F32),
        compiler_params=pltpu.CompilerParams(
            dimension_semantics=("arbitrary", "arbitrary"),
            vmem_limit_bytes=VMEM_LIMIT_BYTES),
        name="fox_out",
    )(o, z2, h1, w_out.astype(BF16), post_gain.reshape(1, D).astype(F32))


def kernel(x, gdn_pre_norm, gdn_w_in, gdn_conv_w, gdn_a_log, gdn_dt_bias, gdn_o_norm, gdn_w_out, gdn_post_norm, kv_norm, kv_w, kv_forget_bias, kv_k_norm, fox_pre_norm, fox_w_in, fox_q_norm, fox_w_out, fox_post_norm):
    assert gdn_w_in.shape[0] == 1 and fox_w_in.shape[0] == 1, "one GDN layer then one FoX layer"
    q, k, v, z, gcol, grow = _gdn_in(x, gdn_pre_norm[0], gdn_w_in[0], gdn_conv_w[0],
                                     gdn_a_log[0], gdn_dt_bias[0])
    o = _gdn_scan(q, k, v, gcol, grow)
    h1, kt2, vs, q2, z2 = _mid(
        o, z, x, gdn_o_norm[0], gdn_w_out[0], gdn_post_norm[0],
        kv_norm, kv_w, kv_forget_bias, kv_k_norm, fox_pre_norm[0], fox_w_in[0], fox_q_norm[0])
    o2 = _fox_attn(q2, kt2, vs)
    return _fox_out(o2, z2, h1, fox_w_out[0], fox_post_norm[0])
```

```python
import functools
import math

import jax
import jax.numpy as jnp
import numpy as np
from jax import lax
from jax.experimental import pallas as pl
from jax.experimental.pallas import tpu as pltpu

HEAD_DIM = 128
N_HEADS = 8
CONV_K = 4
CHUNK = 64
GROUP = 256
PAIR = 128
SCAN_HP = 8
SLAB = 256
ROW_TILE = 512
LOG2E = math.log2(math.e)
EPS = 1e-6
LANES = 128
VMEM_LIMIT_BYTES = 56 * 1024 * 1024
NEG_BIG = -1e30

F32 = jnp.float32
BF16 = jnp.bfloat16


def _dot(a, b):
    return jnp.dot(a, b, preferred_element_type=F32)


def _dot_nt(a, b):
    return lax.dot_general(a, b, (((1,), (1,)), ((), ())), preferred_element_type=F32)


def _dot_tn(a, b):
    return lax.dot_general(a, b, (((0,), (0,)), ((), ())), preferred_element_type=F32)


def _split3(x):
    hi = x.astype(BF16)
    r1 = x - hi.astype(F32)
    mid = r1.astype(BF16)
    lo = (r1 - mid.astype(F32)).astype(BF16)
    return hi, mid, lo


def _dot3(a_bf16, x):
    hi, mid, lo = _split3(x)
    return _dot(a_bf16, hi) + _dot(a_bf16, mid) + _dot(a_bf16, lo)


def _dot3_r(x, b_bf16):
    hi, mid, lo = _split3(x)
    return _dot(hi, b_bf16) + _dot(mid, b_bf16) + _dot(lo, b_bf16)


def _sigmoid(x):
    return 1.0 / (1.0 + jnp.exp(-x))


def _softplus(x):
    return jnp.maximum(x, 0.0) + jnp.log(1.0 + jnp.exp(-jnp.abs(x)))


def _silu(x):
    return x * _sigmoid(x)


def _l2(x):
    return x * lax.rsqrt(jnp.sum(x * x, axis=-1, keepdims=True) + EPS)


def _rms(x, gain):
    ms = jnp.mean(x * x, axis=-1, keepdims=True)
    return x * lax.rsqrt(ms + EPS) * gain


def _resident(shape):
    nd = len(shape)
    return pl.BlockSpec(shape, lambda *_: (0,) * nd, pipeline_mode=pl.Buffered(1))


def _gdn_in_kernel(x_ref, gain_ref, w_ref, wab_ref, wabt_ref, convw_ref, hpc_ref, hpr_ref,
                   lt_ref, ut_ref,
                   q_ref, k_ref, v_ref, z_ref, gcol_ref, grow_ref,
                   xn_scr, pre_scr, carry_scr, *, tm, width, col_chunk):
    @pl.when(pl.program_id(1) == 0)
    def _():
        carry_scr[...] = jnp.zeros_like(carry_scr)

    xn_scr[...] = _rms(x_ref[0], gain_ref[...]).astype(BF16)
    xn = xn_scr[...]

    n_qkv = 3 * width // col_chunk
    for c in range(n_qkv):
        lo = c * col_chunk
        pc = _dot(xn, w_ref[:, lo:lo + col_chunk])
        pre_scr[0:8, :] = carry_scr[:, lo:lo + col_chunk]
        pre_scr[8:8 + tm, :] = pc
        carry_scr[:, lo:lo + col_chunk] = pc[tm - 8:tm, :]
        cw = convw_ref[:, lo:lo + col_chunk]
        acc = pc * cw[CONV_K - 1:CONV_K, :]
        for j in range(1, CONV_K):
            acc = acc + pre_scr[8 - j:8 - j + tm, :] * cw[CONV_K - 1 - j:CONV_K - j, :]
        for hh in range(col_chunk // HEAD_DIM):
            col = lo + hh * HEAD_DIM
            out_ref = (q_ref, k_ref, v_ref)[col // width]
            out_ref[0, (col % width) // HEAD_DIM] = acc[:, hh * HEAD_DIM:(hh + 1) * HEAD_DIM]
    for c in range(width // col_chunk):
        lo = 3 * width + c * col_chunk
        z_ref[0, :, c * col_chunk:(c + 1) * col_chunk] = _dot(xn, w_ref[:, lo:lo + col_chunk]).astype(BF16)

    ab = _dot(xn, wab_ref[...])
    lane = lax.broadcasted_iota(jnp.int32, ab.shape, 1)
    g_col = jnp.where(lane < N_HEADS, hpc_ref[0:1, :] * _softplus(ab + hpc_ref[1:2, :]), 0.0)
    beta_col = _sigmoid(ab)
    cums = _dot3(lt_ref[...], g_col)
    gc_col = cums[0:tm]
    gl_col = pltpu.roll(cums[tm:2 * tm], 2 * N_HEADS, axis=1)
    gates = jnp.where(lane < N_HEADS, gc_col, jnp.where(lane < 2 * N_HEADS, beta_col, gl_col))
    for hg in range(N_HEADS // SCAN_HP):
        shift = (LANES - hg * SCAN_HP) % LANES
        gcol_ref[0, hg] = gates if shift == 0 else pltpu.roll(gates, shift, axis=1)

    abt = _dot_nt(wabt_ref[...], xn)
    row = lax.broadcasted_iota(jnp.int32, abt.shape, 0)
    g_row = jnp.where(row < N_HEADS, hpr_ref[0] * _softplus(abt + hpr_ref[1]), 0.0)
    gc_row = _dot3_r(g_row, ut_ref[...])
    for hg in range(N_HEADS // SCAN_HP):
        grow_ref[0, hg] = gc_row[hg * SCAN_HP:hg * SCAN_HP + 8, :]


def _chunk_tri(tm):
    t = np.arange(tm)
    same = (t[:, None] // CHUNK) == (t[None, :] // CHUNK)
    lower = same & (t[None, :] <= t[:, None])
    return lower.astype(np.float32), same.astype(np.float32)


def _gdn_in(x, pre_gain, w_in, conv_w, a_log, dt_bias, *, tm=GROUP, col_chunk=512):
    B, L, D = x.shape
    W = N_HEADS * HEAD_DIM
    H = N_HEADS
    w_main = w_in[:, :4 * W].astype(BF16)
    wab = jnp.zeros((D, LANES), F32).at[:, :2 * H].set(w_in[:, 4 * W:4 * W + 2 * H]).astype(BF16)
    wabt = w_in[:, 4 * W:4 * W + 2 * H].T.astype(BF16)
    neg_a = -jnp.exp(a_log.astype(F32))
    hpc = jnp.zeros((2, LANES), F32).at[0, :H].set(neg_a).at[1, :H].set(dt_bias.astype(F32))
    hpr = jnp.zeros((2, 2 * H, tm), F32)
    hpr = hpr.at[0, :H, :].set(jnp.broadcast_to(neg_a[:, None], (H, tm)))
    hpr = hpr.at[1, :H, :].set(jnp.broadcast_to(dt_bias.astype(F32)[:, None], (H, tm)))
    lower, same = _chunk_tri(tm)
    lt = jnp.asarray(np.concatenate([lower, same], axis=0), BF16)
    ut = jnp.asarray(lower.T, BF16)

    kern = functools.partial(_gdn_in_kernel, tm=tm, width=W, col_chunk=col_chunk)
    row_spec = lambda n: pl.BlockSpec((1, tm, n), lambda b, i: (b, i, 0))
    head_spec = pl.BlockSpec((1, H, tm, HEAD_DIM), lambda b, i: (b, 0, i, 0))
    out_shape = (
        jax.ShapeDtypeStruct((B, H, L, HEAD_DIM), F32),
        jax.ShapeDtypeStruct((B, H, L, HEAD_DIM), F32),
        jax.ShapeDtypeStruct((B, H, L, HEAD_DIM), F32),
        jax.ShapeDtypeStruct((B, L, W), BF16),
        jax.ShapeDtypeStruct((B, H // SCAN_HP, L, LANES), F32),
        jax.ShapeDtypeStruct((B, H // SCAN_HP, 8, L), F32),
    )
    return pl.pallas_call(
        kern,
        grid=(B, L // tm),
        in_specs=[
            row_spec(D),
            _resident((1, D)),
            _resident((D, 4 * W)),
            _resident((D, LANES)),
            _resident((2 * H, D)),
            _resident((CONV_K, 3 * W)),
            _resident((2, LANES)),
            _resident((2, 2 * H, tm)),
            _resident((2 * tm, tm)),
            _resident((tm, tm)),
        ],
        out_specs=(head_spec, head_spec, head_spec, row_spec(W),
                   pl.BlockSpec((1, H // SCAN_HP, tm, LANES), lambda b, i: (b, 0, i, 0)),
                   pl.BlockSpec((1, H // SCAN_HP, 8, tm), lambda b, i: (b, 0, 0, i))),
        out_shape=out_shape,
        scratch_shapes=[
            pltpu.VMEM((tm, D), BF16),
            pltpu.VMEM((tm + 8, col_chunk), F32),
            pltpu.VMEM((8, 3 * W), F32),
        ],
        compiler_params=pltpu.CompilerParams(
            dimension_semantics=("arbitrary", "arbitrary"),
            vmem_limit_bytes=VMEM_LIMIT_BYTES),
        name="gdn_in",
    )(x, pre_gain.reshape(1, D).astype(F32), w_main, wab, wabt, conv_w.astype(F32), hpc, hpr, lt, ut)


def _gdn_scan_kernel(q_ref, k_ref, v_ref, gcol_ref, grow_ref, cmask_ref, smask_ref, eye_ref,
                     o_ref, s_scr, *, hp):
    P = PAIR
    n_pair = GROUP // P
    cpp = P // CHUNK

    @pl.when(pl.program_id(2) == 0)
    def _():
        s_scr[...] = jnp.zeros_like(s_scr)

    gcol = gcol_ref[0, 0]
    grow = grow_ref[0, 0]
    cmask = cmask_ref[...] > 0
    smask = smask_ref[...]
    eye = eye_ref[...]

    heads = range(hp)

    def setup(j):
        rows = slice(j * P, (j + 1) * P)
        st = dict(glb=[], kbg=[], vb=[], qg=[], kdec=[], attn=[], x=[], p=[])
        for h in heads:
            qf = _l2(_silu(q_ref[0, h, rows, :])) * (HEAD_DIM ** -0.5)
            kf = _l2(_silu(k_ref[0, h, rows, :]))
            vf = _silu(v_ref[0, h, rows, :])
            qh, kh = qf.astype(BF16), kf.astype(BF16)
            gc = jnp.broadcast_to(gcol[rows, h:h + 1], (P, LANES))
            beta = jnp.broadcast_to(gcol[rows, N_HEADS + h:N_HEADS + h + 1], (P, LANES))
            gl = jnp.broadcast_to(gcol[rows, 2 * N_HEADS + h:2 * N_HEADS + h + 1], (P, LANES))
            kb = kf * beta
            egc = jnp.exp(gc)
            st["glb"].append(gl)
            st["kbg"].append((kb * egc).astype(BF16))
            st["vb"].append((vf * beta).astype(BF16))
            st["qg"].append((qf * egc).astype(BF16))
            st["kdec"].append((kf * jnp.exp(gl - gc)).astype(BF16))
            kk = _dot_nt(kb.astype(BF16), kh)
            qk = _dot_nt(qh, kh)
            diff = gc - grow[h:h + 1, rows]
            dec = jnp.exp(jnp.where(cmask, diff, NEG_BIG))
            st["attn"].append((qk * dec).astype(BF16))
            st["x"].append(-(kk * dec * smask))
            st["p"].append(eye + st["x"][-1])
        return st

    def invert(st):
        x, p = st["x"], st["p"]
        xb = [x[h].astype(BF16) for h in heads]
        x = [_dot(xb[h], xb[h]) for h in heads]
        yield
        for _ in range(4):
            xb = [x[h].astype(BF16) for h in heads]
            r = [_dot(jnp.concatenate([xb[h], p[h].astype(BF16)], axis=0), xb[h]) for h in heads]
            x = [r[h][:P] for h in heads]
            p = [p[h] + r[h][P:] for h in heads]
            yield
        p = [p[h] + _dot(p[h].astype(BF16), x[h].astype(BF16)) for h in heads]
        yield
        uw = [_dot(p[h].astype(BF16), jnp.concatenate([st["vb"][h], st["kbg"][h]], axis=1)) for h in heads]
        st["u"] = [uw[h][:, :HEAD_DIM] for h in heads]
        st["wm"] = [uw[h][:, HEAD_DIM:].astype(BF16) for h in heads]
        yield

    def advance(st, j, s):
        vn_parts = [[] for _ in heads]
        for c in range(cpp):
            r0 = c * CHUNK
            wq = [_dot(jnp.concatenate([st["wm"][h][r0:r0 + CHUNK], st["qg"][h][r0:r0 + CHUNK]], axis=0),
                       s[h].astype(BF16)) for h in heads]
            yield
            vn = [(st["u"][h][r0:r0 + CHUNK] - wq[h][:CHUNK]).astype(BF16) for h in heads]
            pad = [jnp.zeros(((cpp - 1 - c) * CHUNK, HEAD_DIM), BF16)] if c < cpp - 1 else []
            for h in heads:
                vn_parts[h].append(vn[h])
                vn_all = jnp.concatenate(vn_parts[h] + pad, axis=0)
                o_c = wq[h][CHUNK:] + _dot(st["attn"][h][r0:r0 + CHUNK, :], vn_all)
                o_ref[0, h, j * P + r0:j * P + r0 + CHUNK, :] = o_c.astype(o_ref.dtype)
            yield
            for h in heads:
                s[h] = (s[h] * jnp.exp(st["glb"][h][r0:r0 + 1, :])
                        + _dot_tn(st["kdec"][h][r0:r0 + CHUNK], vn[h]))
            yield

    s = [s_scr[h] for h in heads]
    prev = None
    for j in range(n_pair):
        st = setup(j)
        gens = [invert(st)] + ([advance(prev, j - 1, s)] if prev is not None else [])
        while gens:
            for g in list(gens):
                if next(g, "done") == "done":
                    gens.remove(g)
        prev = st
    for _ in advance(prev, n_pair - 1, s):
        pass
    for h in heads:
        s_scr[h] = s[h]


def _pair_masks():
    t = np.arange(PAIR)
    same = (t[:, None] // CHUNK) == (t[None, :] // CHUNK)
    causal = same & (t[None, :] <= t[:, None])
    strict = same & (t[None, :] < t[:, None])
    return causal.astype(np.float32), strict.astype(np.float32), np.eye(PAIR, dtype=np.float32)


def _gdn_scan(q, k, v, gcol, grow):
    B, H, L, Dh = q.shape
    hp = SCAN_HP
    causal, strict, eye = _pair_masks()
    head_spec = pl.BlockSpec((1, hp, GROUP, Dh), lambda b, hg, g: (b, hg, g, 0))
    return pl.pallas_call(
        functools.partial(_gdn_scan_kernel, hp=hp),
        grid=(B, H // hp, L // GROUP),
        in_specs=[
            head_spec, head_spec, head_spec,
            pl.BlockSpec((1, 1, GROUP, LANES), lambda b, hg, g: (b, hg, g, 0)),
            pl.BlockSpec((1, 1, 8, GROUP), lambda b, hg, g: (b, hg, 0, g)),
            _resident((PAIR, PAIR)), _resident((PAIR, PAIR)), _resident((PAIR, PAIR)),
        ],
        out_specs=head_spec,
        out_shape=jax.ShapeDtypeStruct((B, H, L, Dh), BF16),
        scratch_shapes=[pltpu.VMEM((hp, Dh, Dh), F32)],
        compiler_params=pltpu.CompilerParams(
            dimension_semantics=("arbitrary", "arbitrary", "arbitrary"),
            vmem_limit_bytes=VMEM_LIMIT_BYTES),
        name="gdn_scan",
    )(q, k, v, gcol, grow, jnp.asarray(causal), jnp.asarray(strict), jnp.asarray(eye))


def _head_rms(xh, gain):
    ms = jnp.mean(xh * xh, axis=-1, keepdims=True)
    return xh * lax.rsqrt(ms + EPS) * gain


def _mid_kernel(o_ref, z_ref, x_ref, onorm_ref, wo_ref, post_ref,
                kvn_ref, wkt_ref, wv_ref, wf_ref, wft_ref, fb_ref, fbr_ref, kkn_ref,
                fpre_ref, wq_ref, wz_ref, qn_ref, lt_ref, ut_ref, selq_ref, qc_ref,
                h1_ref, kt_ref, v_ref, q_ref, z2_ref,
                carry_scr, carry_r_scr, *, tm):
    @pl.when(pl.program_id(1) == 0)
    def _():
        carry_scr[...] = jnp.zeros_like(carry_scr)
        carry_r_scr[...] = jnp.zeros_like(carry_r_scr)

    gated = []
    for h in range(N_HEADS):
        on = _head_rms(o_ref[0, h].astype(F32), onorm_ref[...])
        zh = z_ref[0, :, h * HEAD_DIM:(h + 1) * HEAD_DIM].astype(F32)
        gated.append((on * (zh * _sigmoid(zh))).astype(BF16))
    y = _dot(jnp.concatenate(gated, axis=1), wo_ref[...])
    h1 = x_ref[0] + _rms(y, post_ref[...])
    h1_ref[0] = h1

    u = _rms(h1, kvn_ref[...]).astype(BF16)
    vv = _dot(u, wv_ref[...])
    for h in range(N_HEADS):
        v_ref[0, h] = vv[:, h * HEAD_DIM:(h + 1) * HEAD_DIM].astype(BF16)

    f = _dot(u, wf_ref[...]) + fb_ref[...]
    lane = lax.broadcasted_iota(jnp.int32, f.shape, 1)
    logf = jnp.where(lane < N_HEADS, -_softplus(-f), 0.0)
    c = _dot3(lt_ref[...], logf) + carry_scr[0:1, :]
    carry_scr[...] = jnp.broadcast_to(c[tm - 1:tm, :], carry_scr.shape)
    c_hi, c_mid, c_lo = _split3(c * LOG2E)
    packed = (c_hi.astype(F32) + pltpu.roll(c_mid.astype(F32), N_HEADS, axis=1)
              + pltpu.roll(c_lo.astype(F32), 2 * N_HEADS, axis=1)).astype(BF16)
    qaug = _dot(packed, selq_ref[...]) + qc_ref[...]

    ft = _dot_nt(wft_ref[...], u) + fbr_ref[...]
    row = lax.broadcasted_iota(jnp.int32, ft.shape, 0)
    logf_r = jnp.where(row < N_HEADS, -_softplus(-ft), 0.0)
    c_r = _dot3_r(logf_r, ut_ref[...]) + carry_r_scr[...]
    carry_r_scr[...] = jnp.broadcast_to(c_r[:, tm - 1:tm], carry_r_scr.shape)
    r_hi, r_mid, r_lo = _split3(c_r * LOG2E)
    r_hi, r_mid, r_lo = r_hi.astype(F32), r_mid.astype(F32), r_lo.astype(F32)

    kt = _dot_nt(wkt_ref[...], u)
    sub = lax.broadcasted_iota(jnp.int32, (16, tm), 0)
    for h in range(N_HEADS):
        kh = kt[h * HEAD_DIM:(h + 1) * HEAD_DIM, :]
        ms = jnp.mean(kh * kh, axis=0, keepdims=True)
        khn = (kh * lax.rsqrt(ms + EPS) * kkn_ref[...]).astype(BF16)
        aug = jnp.where(sub < 3, 1.0,
                        jnp.where(sub == 3, -r_hi[h:h + 1],
                                  jnp.where(sub == 4, -r_mid[h:h + 1],
                                            jnp.where(sub == 5, -r_lo[h:h + 1], 0.0)))).astype(BF16)
        for sl in range(tm // SLAB):
            cols = slice(sl * SLAB, (sl + 1) * SLAB)
            kt_ref[0, h, sl, 0:HEAD_DIM, :] = khn[:, cols]
            kt_ref[0, h, sl, HEAD_DIM:HEAD_DIM + 16, :] = aug[:, cols]
            kt_ref[0, h, sl, HEAD_DIM + 16:2 * HEAD_DIM, :] = jnp.zeros((HEAD_DIM - 16, SLAB), BF16)

    u2 = _rms(h1, fpre_ref[...]).astype(BF16)
    qq = _dot(u2, wq_ref[...])
    q_scale = HEAD_DIM ** -0.5 * LOG2E
    for h in range(N_HEADS):
        qh = _head_rms(qq[:, h * HEAD_DIM:(h + 1) * HEAD_DIM], qn_ref[...])
        q_ref[0, h, :, 0:HEAD_DIM] = (qh * q_scale).astype(BF16)
        q_ref[0, h, :, HEAD_DIM:2 * HEAD_DIM] = qaug[:, h * HEAD_DIM:(h + 1) * HEAD_DIM].astype(BF16)
    z2_ref[0] = _dot(u2, wz_ref[...]).astype(BF16)


def _aug_selectors():
    selq = np.zeros((LANES, N_HEADS * HEAD_DIM), np.float32)
    qc = np.zeros((1, N_HEADS * HEAD_DIM), np.float32)
    for h in range(N_HEADS):
        for t in range(3):
            selq[t * N_HEADS + h, h * HEAD_DIM + t] = 1.0
            qc[0, h * HEAD_DIM + 3 + t] = 1.0
    return selq, qc


def _mid(o, z, x, o_norm, w_out, post_gain, kv_norm, kv_w, kv_fbias, kv_k_norm,
         fox_pre, fox_w_in, fox_q_norm, *, tm=ROW_TILE):
    B, L, D = x.shape
    H, W = N_HEADS, N_HEADS * HEAD_DIM
    assert tm % SLAB == 0
    wo = w_out.astype(BF16)
    wkt = kv_w[:, :W].T.astype(BF16)
    wv = kv_w[:, W:2 * W].astype(BF16)
    wf = jnp.zeros((D, LANES), F32).at[:, :H].set(kv_w[:, 2 * W:2 * W + H]).astype(BF16)
    wft = jnp.zeros((2 * H, D), F32).at[:H, :].set(kv_w[:, 2 * W:2 * W + H].T).astype(BF16)
    fb = jnp.zeros((1, LANES), F32).at[0, :H].set(kv_fbias.astype(F32))
    fbr = jnp.zeros((2 * H, tm), F32).at[:H, :].set(
        jnp.broadcast_to(kv_fbias.astype(F32)[:, None], (H, tm)))
    kkn = jnp.broadcast_to(kv_k_norm.astype(F32)[:, None], (HEAD_DIM, tm))
    wq = fox_w_in[:, :W].astype(BF16)
    wz = fox_w_in[:, W:2 * W].astype(BF16)
    tri = np.tril(np.ones((tm, tm), np.float32))
    selq, qc = _aug_selectors()
    row = lambda n: pl.BlockSpec((1, tm, n), lambda b, i: (b, i, 0))
    head_spec = lambda n: pl.BlockSpec((1, H, tm, n), lambda b, i: (b, 0, i, 0))
    vec = lambda a, n: a.reshape(1, n).astype(F32)
    out_shape = (
        jax.ShapeDtypeStruct((B, L, D), F32),
        jax.ShapeDtypeStruct((B, H, L // SLAB, 2 * HEAD_DIM, SLAB), BF16),
        jax.ShapeDtypeStruct((B, H, L, HEAD_DIM), BF16),
        jax.ShapeDtypeStruct((B, H, L, 2 * HEAD_DIM), BF16),
        jax.ShapeDtypeStruct((B, L, W), BF16),
    )
    return pl.pallas_call(
        functools.partial(_mid_kernel, tm=tm),
        grid=(B, L // tm),
        in_specs=[
            head_spec(HEAD_DIM), row(W), row(D),
            _resident((1, HEAD_DIM)), _resident((W, D)), _resident((1, D)),
            _resident((1, D)), _resident((W, D)), _resident((D, W)), _resident((D, LANES)),
            _resident((2 * H, D)), _resident((1, LANES)), _resident((2 * H, tm)),
            _resident((HEAD_DIM, tm)),
            _resident((1, D)), _resident((D, W)), _resident((D, W)), _resident((1, HEAD_DIM)),
            _resident((tm, tm)), _resident((tm, tm)), _resident((LANES, W)), _resident((1, W)),
        ],
        out_specs=(row(D),
                   pl.BlockSpec((1, H, tm // SLAB, 2 * HEAD_DIM, SLAB), lambda b, i: (b, 0, i, 0, 0)),
                   head_spec(HEAD_DIM), head_spec(2 * HEAD_DIM), row(W)),
        out_shape=out_shape,
        scratch_shapes=[pltpu.VMEM((8, LANES), F32), pltpu.VMEM((2 * H, tm), F32)],
        compiler_params=pltpu.CompilerParams(
            dimension_semantics=("arbitrary", "arbitrary"),
            vmem_limit_bytes=VMEM_LIMIT_BYTES),
        name="mid",
    )(o, z, x, vec(o_norm, HEAD_DIM), wo, vec(post_gain, D),
      vec(kv_norm, D), wkt, wv, wf, wft, fb, fbr, kkn,
      vec(fox_pre, D), wq, wz, vec(fox_q_norm, HEAD_DIM),
      jnp.asarray(tri, BF16), jnp.asarray(tri.T, BF16), jnp.asarray(selq, BF16), jnp.asarray(qc))


def _fox_attn_kernel(q_ref, kt_ref, v_ref, o_ref, acc_scr, m_scr, *, tq, hp):
    ts = SLAB
    i = pl.program_id(2)
    n_diag = tq // ts
    lane = lax.broadcasted_iota(jnp.int32, (ts, HEAD_DIM), 1)
    ones_col = jnp.where(lane == 0, 1.0, 0.0).astype(BF16)
    col = lax.broadcasted_iota(jnp.int32, (2 * ts, ts), 1)
    rowi = lax.broadcasted_iota(jnp.int32, (2 * ts, ts), 0)
    pair_masks = (col <= rowi, col + ts <= rowi)

    def v_aug(h, j):
        r0 = pl.multiple_of(j * ts, ts)
        return jnp.concatenate([v_ref[0, h, pl.ds(r0, ts), :], ones_col], axis=1)

    def scores(h, slabs, row0):
        q = q_ref[0, h, row0:tq, :]
        return [_dot(q, kt_ref[0, h, j]) for j in slabs]

    def finish(h, parts, slabs, row0, diag):
        if diag:
            parts = [jnp.concatenate([jnp.where(mk, p[:2 * ts], NEG_BIG), p[2 * ts:]], axis=0)
                     if p.shape[0] > 2 * ts else jnp.where(mk, p, NEG_BIG)
                     for p, mk in zip(parts, pair_masks)]
        mx = parts[0]
        for p in parts[1:]:
            mx = jnp.maximum(mx, p)
        m_old = m_scr[h, row0:tq, :]
        m_new = jnp.maximum(m_old, jnp.max(mx, axis=1, keepdims=True))
        m_scr[h, row0:tq, :] = m_new
        alpha = jnp.exp2(m_old - m_new)
        m_wide = jnp.concatenate([m_new, m_new], axis=1)
        pv = None
        for p, j in zip(parts, slabs):
            d = _dot(jnp.exp2(p - m_wide).astype(BF16), v_aug(h, j))
            pv = d if pv is None else pv + d
        acc_scr[h, row0:tq, :] = jnp.concatenate([alpha, alpha], axis=1) * acc_scr[h, row0:tq, :] + pv

    acc_scr[...] = jnp.zeros_like(acc_scr)
    m_scr[...] = jnp.full(m_scr.shape, NEG_BIG, F32)

    def step(slabs, row0, diag):
        parts = [scores(h, slabs, row0) for h in range(hp)]
        for h in range(hp):
            finish(h, parts[h], slabs, row0, diag)

    def full_blocks(jj, carry):
        step((2 * jj, 2 * jj + 1), 0, False)
        return carry

    lax.fori_loop(0, i * (n_diag // 2), full_blocks, 0)
    for d in range(n_diag // 2):
        step((i * n_diag + 2 * d, i * n_diag + 2 * d + 1), 2 * d * ts, True)
    for h in range(hp):
        acc = acc_scr[h]
        o_ref[0, h] = (acc[:, :HEAD_DIM] / acc[:, HEAD_DIM:HEAD_DIM + 1]).astype(o_ref.dtype)


def _fox_attn(q2, kt2, v, *, tq=1024, hp=4):
    B, H, L, Dh = v.shape
    tq = min(tq, L)
    assert tq % (2 * SLAB) == 0 and L % tq == 0 and H % hp == 0
    return pl.pallas_call(
        functools.partial(_fox_attn_kernel, tq=tq, hp=hp),
        grid=(B, H // hp, L // tq),
        in_specs=[pl.BlockSpec((1, hp, tq, 2 * Dh), lambda b, h, i: (b, h, i, 0)),
                  pl.BlockSpec((1, hp, L // SLAB, 2 * Dh, SLAB), lambda b, h, i: (b, h, 0, 0, 0)),
                  pl.BlockSpec((1, hp, L, Dh), lambda b, h, i: (b, h, 0, 0))],
        out_specs=pl.BlockSpec((1, hp, tq, Dh), lambda b, h, i: (b, h, i, 0)),
        out_shape=jax.ShapeDtypeStruct((B, H, L, Dh), BF16),
        scratch_shapes=[pltpu.VMEM((hp, tq, 2 * Dh), F32), pltpu.VMEM((hp, tq, LANES), F32)],
        compiler_params=pltpu.CompilerParams(
            dimension_semantics=("arbitrary", "arbitrary", "arbitrary"),
            vmem_limit_bytes=VMEM_LIMIT_BYTES),
        name="fox_attn",
    )(q2, kt2, v)


def _fox_out_kernel(o_ref, z_ref, h_ref, wo_ref, post_ref, out_ref, *, tm, n_split):
    rb = tm // n_split
    gated = []
    for r in range(n_split):
        rows = slice(r * rb, (r + 1) * rb)
        parts = []
        for h in range(N_HEADS):
            zh = z_ref[0, rows, h * HEAD_DIM:(h + 1) * HEAD_DIM].astype(F32)
            parts.append((o_ref[0, h, rows, :].astype(F32) * (zh * _sigmoid(zh))).astype(BF16))
        gated.append(jnp.concatenate(parts, axis=1))
    y = [_dot(gated[r], wo_ref[...]) for r in range(n_split)]
    for r in range(n_split):
        rows = slice(r * rb, (r + 1) * rb)
        out_ref[0, rows, :] = h_ref[0, rows, :] + _rms(y[r], post_ref[...])


def _fox_out(o, z2, h1, w_out, post_gain, *, tm=ROW_TILE):
    B, L, D = h1.shape
    H, W = N_HEADS, N_HEADS * HEAD_DIM
    row = lambda n: pl.BlockSpec((1, tm, n), lambda b, i: (b, i, 0))
    return pl.pallas_call(
        functools.partial(_fox_out_kernel, tm=tm, n_split=1),
        grid=(B, L // tm),
        in_specs=[pl.BlockSpec((1, H, tm, HEAD_DIM), lambda b, i: (b, 0, i, 0)), row(W), row(D),
                  _resident((W, D)), _resident((1, D))],
        out_specs=row(D),
        out_shape=jax.ShapeDtypeStruct((B, L, D), F32),
        compiler_params=pltpu.CompilerParams(
            dimension_semantics=("arbitrary", "arbitrary"),
            vmem_limit_bytes=VMEM_LIMIT_BYTES),
        name="fox_out",
    )(o, z2, h1, w_out.astype(BF16), post_gain.reshape(1, D).astype(F32))


def kernel(x, gdn_pre_norm, gdn_w_in, gdn_conv_w, gdn_a_log, gdn_dt_bias, gdn_o_norm, gdn_w_out, gdn_post_norm, kv_norm, kv_w, kv_forget_bias, kv_k_norm, fox_pre_norm, fox_w_in, fox_q_norm, fox_w_out, fox_post_norm):
    assert gdn_w_in.shape[0] == 1 and fox_w_in.shape[0] == 1, "one GDN layer then one FoX layer"
    q, k, v, z, gcol, grow = _gdn_in(x, gdn_pre_norm[0], gdn_w_in[0], gdn_conv_w[0],
                                     gdn_a_log[0], gdn_dt_bias[0])
    o = _gdn_scan(q, k, v, gcol, grow)
    h1, kt2, vs, q2, z2 = _mid(
        o, z, x, gdn_o_norm[0], gdn_w_out[0], gdn_post_norm[0],
        kv_norm, kv_w, kv_forget_bias, kv_k_norm, fox_pre_norm[0], fox_w_in[0], fox_q_norm[0])
    o2 = _fox_attn(q2, kt2, vs)
    return _fox_out(o2, z2, h1, fox_w_out[0], fox_post_norm[0])
```

```python
import functools
import math

import jax
import jax.numpy as jnp
import numpy as np
from jax import lax
from jax.experimental import pallas as pl
from jax.experimental.pallas import tpu as pltpu

HEAD_DIM = 128
N_HEADS = 8
CONV_K = 4
CHUNK = 64
GROUP = 256
PAIR = 128
SCAN_HP = 8
SLAB = 256
ROW_TILE = 512
LOG2E = math.log2(math.e)
EPS = 1e-6
LANES = 128
VMEM_LIMIT_BYTES = 56 * 1024 * 1024
NEG_BIG = -1e30

F32 = jnp.float32
BF16 = jnp.bfloat16


def _dot(a, b):
    return jnp.dot(a, b, preferred_element_type=F32)


def _dot_nt(a, b):
    return lax.dot_general(a, b, (((1,), (1,)), ((), ())), preferred_element_type=F32)


def _dot_tn(a, b):
    return lax.dot_general(a, b, (((0,), (0,)), ((), ())), preferred_element_type=F32)


def _split3(x):
    hi = x.astype(BF16)
    r1 = x - hi.astype(F32)
    mid = r1.astype(BF16)
    lo = (r1 - mid.astype(F32)).astype(BF16)
    return hi, mid, lo


def _dot3(a_bf16, x):
    hi, mid, lo = _split3(x)
    return _dot(a_bf16, hi) + _dot(a_bf16, mid) + _dot(a_bf16, lo)


def _dot3_r(x, b_bf16):
    hi, mid, lo = _split3(x)
    return _dot(hi, b_bf16) + _dot(mid, b_bf16) + _dot(lo, b_bf16)


def _sigmoid(x):
    return 1.0 / (1.0 + jnp.exp(-x))


def _softplus(x):
    return jnp.maximum(x, 0.0) + jnp.log(1.0 + jnp.exp(-jnp.abs(x)))


def _silu(x):
    return x * _sigmoid(x)


def _l2(x):
    return x * lax.rsqrt(jnp.sum(x * x, axis=-1, keepdims=True) + EPS)


def _rms(x, gain):
    ms = jnp.mean(x * x, axis=-1, keepdims=True)
    return x * lax.rsqrt(ms + EPS) * gain


def _resident(shape):
    nd = len(shape)
    return pl.BlockSpec(shape, lambda *_: (0,) * nd, pipeline_mode=pl.Buffered(1))


def _gdn_in_kernel(x_ref, gain_ref, w_ref, wab_ref, wabt_ref, convw_ref, hpc_ref, hpr_ref,
                   lt_ref, ut_ref,
                   q_ref, k_ref, v_ref, z_ref, gcol_ref, grow_ref,
                   xn_scr, pre_scr, carry_scr, *, tm, width, col_chunk):
    @pl.when(pl.program_id(1) == 0)
    def _():
        carry_scr[...] = jnp.zeros_like(carry_scr)

    xn_scr[...] = _rms(x_ref[0], gain_ref[...]).astype(BF16)
    xn = xn_scr[...]

    def z_chunk(c):
        lo = 3 * width + c * col_chunk
        z_ref[0, :, c * col_chunk:(c + 1) * col_chunk] = _dot(xn, w_ref[:, lo:lo + col_chunk]).astype(BF16)

    n_qkv = 3 * width // col_chunk
    n_z = width // col_chunk
    z_after = {(i + 1) * n_qkv // (n_z + 1) - 1: i for i in range(n_z)}

    def conv_chunk(c):
        lo = c * col_chunk
        pc = _dot(xn, w_ref[:, lo:lo + col_chunk])
        pre_scr[0:8, :] = carry_scr[:, lo:lo + col_chunk]
        pre_scr[8:8 + tm, :] = pc
        carry_scr[:, lo:lo + col_chunk] = pc[tm - 8:tm, :]
        cw = convw_ref[:, lo:lo + col_chunk]
        acc = pc * cw[CONV_K - 1:CONV_K, :]
        for j in range(1, CONV_K):
            acc = acc + pre_scr[8 - j:8 - j + tm, :] * cw[CONV_K - 1 - j:CONV_K - j, :]
        for hh in range(col_chunk // HEAD_DIM):
            col = lo + hh * HEAD_DIM
            out_ref = (q_ref, k_ref, v_ref)[col // width]
            out_ref[0, (col % width) // HEAD_DIM] = acc[:, hh * HEAD_DIM:(hh + 1) * HEAD_DIM]
        if c in z_after:
            z_chunk(z_after[c])

    for c in range(n_qkv // 2):
        conv_chunk(c)

    ab = _dot(xn, wab_ref[...])
    lane = lax.broadcasted_iota(jnp.int32, ab.shape, 1)
    g_col = jnp.where(lane < N_HEADS, hpc_ref[0:1, :] * _softplus(ab + hpc_ref[1:2, :]), 0.0)
    beta_col = _sigmoid(ab)
    cums = _dot3(lt_ref[...], g_col)
    gc_col = cums[0:tm]
    gl_col = pltpu.roll(cums[tm:2 * tm], 2 * N_HEADS, axis=1)
    gates = jnp.where(lane < N_HEADS, gc_col, jnp.where(lane < 2 * N_HEADS, beta_col, gl_col))
    for hg in range(N_HEADS // SCAN_HP):
        shift = (LANES - hg * SCAN_HP) % LANES
        gcol_ref[0, hg] = gates if shift == 0 else pltpu.roll(gates, shift, axis=1)

    abt = _dot_nt(wabt_ref[...], xn)
    row = lax.broadcasted_iota(jnp.int32, abt.shape, 0)
    g_row = jnp.where(row < N_HEADS, hpr_ref[0] * _softplus(abt + hpr_ref[1]), 0.0)
    gc_row = _dot3_r(g_row, ut_ref[...])
    for hg in range(N_HEADS // SCAN_HP):
        grow_ref[0, hg] = gc_row[hg * SCAN_HP:hg * SCAN_HP + 8, :]

    for c in range(n_qkv // 2, n_qkv):
        conv_chunk(c)


def _chunk_tri(tm):
    t = np.arange(tm)
    same = (t[:, None] // CHUNK) == (t[None, :] // CHUNK)
    lower = same & (t[None, :] <= t[:, None])
    return lower.astype(np.float32), same.astype(np.float32)


def _gdn_in(x, pre_gain, w_in, conv_w, a_log, dt_bias, *, tm=GROUP, col_chunk=256):
    B, L, D = x.shape
    W = N_HEADS * HEAD_DIM
    H = N_HEADS
    w_main = w_in[:, :4 * W].astype(BF16)
    wab = jnp.zeros((D, LANES), F32).at[:, :2 * H].set(w_in[:, 4 * W:4 * W + 2 * H]).astype(BF16)
    wabt = w_in[:, 4 * W:4 * W + 2 * H].T.astype(BF16)
    neg_a = -jnp.exp(a_log.astype(F32))
    hpc = jnp.zeros((2, LANES), F32).at[0, :H].set(neg_a).at[1, :H].set(dt_bias.astype(F32))
    hpr = jnp.zeros((2, 2 * H, tm), F32)
    hpr = hpr.at[0, :H, :].set(jnp.broadcast_to(neg_a[:, None], (H, tm)))
    hpr = hpr.at[1, :H, :].set(jnp.broadcast_to(dt_bias.astype(F32)[:, None], (H, tm)))
    lower, same = _chunk_tri(tm)
    lt = jnp.asarray(np.concatenate([lower, same], axis=0), BF16)
    ut = jnp.asarray(lower.T, BF16)

    kern = functools.partial(_gdn_in_kernel, tm=tm, width=W, col_chunk=col_chunk)
    row_spec = lambda n: pl.BlockSpec((1, tm, n), lambda b, i: (b, i, 0))
    head_spec = pl.BlockSpec((1, H, tm, HEAD_DIM), lambda b, i: (b, 0, i, 0))
    out_shape = (
        jax.ShapeDtypeStruct((B, H, L, HEAD_DIM), F32),
        jax.ShapeDtypeStruct((B, H, L, HEAD_DIM), F32),
        jax.ShapeDtypeStruct((B, H, L, HEAD_DIM), F32),
        jax.ShapeDtypeStruct((B, L, W), BF16),
        jax.ShapeDtypeStruct((B, H // SCAN_HP, L, LANES), F32),
        jax.ShapeDtypeStruct((B, H // SCAN_HP, 8, L), F32),
    )
    return pl.pallas_call(
        kern,
        grid=(B, L // tm),
        in_specs=[
            row_spec(D),
            _resident((1, D)),
            _resident((D, 4 * W)),
            _resident((D, LANES)),
            _resident((2 * H, D)),
            _resident((CONV_K, 3 * W)),
            _resident((2, LANES)),
            _resident((2, 2 * H, tm)),
            _resident((2 * tm, tm)),
            _resident((tm, tm)),
        ],
        out_specs=(head_spec, head_spec, head_spec, row_spec(W),
                   pl.BlockSpec((1, H // SCAN_HP, tm, LANES), lambda b, i: (b, 0, i, 0)),
                   pl.BlockSpec((1, H // SCAN_HP, 8, tm), lambda b, i: (b, 0, 0, i))),
        out_shape=out_shape,
        scratch_shapes=[
            pltpu.VMEM((tm, D), BF16),
            pltpu.VMEM((tm + 8, col_chunk), F32),
            pltpu.VMEM((8, 3 * W), F32),
        ],
        compiler_params=pltpu.CompilerParams(
            dimension_semantics=("arbitrary", "arbitrary"),
            vmem_limit_bytes=VMEM_LIMIT_BYTES),
        name="gdn_in",
    )(x, pre_gain.reshape(1, D).astype(F32), w_main, wab, wabt, conv_w.astype(F32), hpc, hpr, lt, ut)


def _gdn_scan_kernel(q_ref, k_ref, v_ref, gcol_ref, grow_ref, cmask_ref, smask_ref, eye_ref,
                     o_ref, s_scr, *, hp):
    P = PAIR
    n_pair = GROUP // P
    cpp = P // CHUNK

    @pl.when(pl.program_id(2) == 0)
    def _():
        s_scr[...] = jnp.zeros_like(s_scr)

    gcol = gcol_ref[0, 0]
    grow = grow_ref[0, 0]
    cmask = cmask_ref[...] > 0
    smask = smask_ref[...]
    eye = eye_ref[...]

    heads = range(hp)

    def setup(j):
        rows = slice(j * P, (j + 1) * P)
        st = dict(glb=[], kbg=[], vb=[], qg=[], kdec=[], attn=[], x=[], p=[])
        for h in heads:
            qf = _l2(_silu(q_ref[0, h, rows, :])) * (HEAD_DIM ** -0.5)
            kf = _l2(_silu(k_ref[0, h, rows, :]))
            vf = _silu(v_ref[0, h, rows, :])
            qh, kh = qf.astype(BF16), kf.astype(BF16)
            gc = jnp.broadcast_to(gcol[rows, h:h + 1], (P, LANES))
            beta = jnp.broadcast_to(gcol[rows, N_HEADS + h:N_HEADS + h + 1], (P, LANES))
            gl = jnp.broadcast_to(gcol[rows, 2 * N_HEADS + h:2 * N_HEADS + h + 1], (P, LANES))
            kb = kf * beta
            egc = jnp.exp(gc)
            st["glb"].append(gl)
            st["kbg"].append((kb * egc).astype(BF16))
            st["vb"].append((vf * beta).astype(BF16))
            st["qg"].append((qf * egc).astype(BF16))
            st["kdec"].append((kf * jnp.exp(gl - gc)).astype(BF16))
            kk = _dot_nt(kb.astype(BF16), kh)
            qk = _dot_nt(qh, kh)
            diff = gc - grow[h:h + 1, rows]
            dec = jnp.exp(jnp.where(cmask, diff, NEG_BIG))
            st["attn"].append((qk * dec).astype(BF16))
            st["x"].append(-(kk * dec * smask))
            st["p"].append(eye + st["x"][-1])
        return st

    def invert(st):
        x, p = st["x"], st["p"]
        xb = [x[h].astype(BF16) for h in heads]
        x = [_dot(xb[h], xb[h]) for h in heads]
        yield
        for _ in range(4):
            xb = [x[h].astype(BF16) for h in heads]
            r = [_dot(jnp.concatenate([xb[h], p[h].astype(BF16)], axis=0), xb[h]) for h in heads]
            x = [r[h][:P] for h in heads]
            p = [p[h] + r[h][P:] for h in heads]
            yield
        p = [p[h] + _dot(p[h].astype(BF16), x[h].astype(BF16)) for h in heads]
        yield
        uw = [_dot(p[h].astype(BF16), jnp.concatenate([st["vb"][h], st["kbg"][h]], axis=1)) for h in heads]
        st["u"] = [uw[h][:, :HEAD_DIM] for h in heads]
        st["wm"] = [uw[h][:, HEAD_DIM:].astype(BF16) for h in heads]
        yield

    def advance(st, j, s):
        vn_parts = [[] for _ in heads]
        for c in range(cpp):
            r0 = c * CHUNK
            wq = [_dot(jnp.concatenate([st["wm"][h][r0:r0 + CHUNK], st["qg"][h][r0:r0 + CHUNK]], axis=0),
                       s[h].astype(BF16)) for h in heads]
            yield
            vn = [(st["u"][h][r0:r0 + CHUNK] - wq[h][:CHUNK]).astype(BF16) for h in heads]
            pad = [jnp.zeros(((cpp - 1 - c) * CHUNK, HEAD_DIM), BF16)] if c < cpp - 1 else []
            for h in heads:
                vn_parts[h].append(vn[h])
                vn_all = jnp.concatenate(vn_parts[h] + pad, axis=0)
                o_c = wq[h][CHUNK:] + _dot(st["attn"][h][r0:r0 + CHUNK, :], vn_all)
                o_ref[0, h, j * P + r0:j * P + r0 + CHUNK, :] = o_c.astype(o_ref.dtype)
            yield
            for h in heads:
                s[h] = (s[h] * jnp.exp(st["glb"][h][r0:r0 + 1, :])
                        + _dot_tn(st["kdec"][h][r0:r0 + CHUNK], vn[h]))
            yield

    s = [s_scr[h] for h in heads]
    prev = None
    for j in range(n_pair):
        st = setup(j)
        gens = [invert(st)] + ([advance(prev, j - 1, s)] if prev is not None else [])
        while gens:
            for g in list(gens):
                if next(g, "done") == "done":
                    gens.remove(g)
        prev = st
    for _ in advance(prev, n_pair - 1, s):
        pass
    for h in heads:
        s_scr[h] = s[h]


def _pair_masks():
    t = np.arange(PAIR)
    same = (t[:, None] // CHUNK) == (t[None, :] // CHUNK)
    causal = same & (t[None, :] <= t[:, None])
    strict = same & (t[None, :] < t[:, None])
    return causal.astype(np.float32), strict.astype(np.float32), np.eye(PAIR, dtype=np.float32)


def _gdn_scan(q, k, v, gcol, grow):
    B, H, L, Dh = q.shape
    hp = SCAN_HP
    causal, strict, eye = _pair_masks()
    head_spec = pl.BlockSpec((1, hp, GROUP, Dh), lambda b, hg, g: (b, hg, g, 0))
    return pl.pallas_call(
        functools.partial(_gdn_scan_kernel, hp=hp),
        grid=(B, H // hp, L // GROUP),
        in_specs=[
            head_spec, head_spec, head_spec,
            pl.BlockSpec((1, 1, GROUP, LANES), lambda b, hg, g: (b, hg, g, 0)),
            pl.BlockSpec((1, 1, 8, GROUP), lambda b, hg, g: (b, hg, 0, g)),
            _resident((PAIR, PAIR)), _resident((PAIR, PAIR)), _resident((PAIR, PAIR)),
        ],
        out_specs=head_spec,
        out_shape=jax.ShapeDtypeStruct((B, H, L, Dh), BF16),
        scratch_shapes=[pltpu.VMEM((hp, Dh, Dh), F32)],
        compiler_params=pltpu.CompilerParams(
            dimension_semantics=("arbitrary", "arbitrary", "arbitrary"),
            vmem_limit_bytes=VMEM_LIMIT_BYTES),
        name="gdn_scan",
    )(q, k, v, gcol, grow, jnp.asarray(causal), jnp.asarray(strict), jnp.asarray(eye))


def _head_rms(xh, gain):
    ms = jnp.mean(xh * xh, axis=-1, keepdims=True)
    return xh * lax.rsqrt(ms + EPS) * gain


def _mid_kernel(o_ref, z_ref, x_ref, onorm_ref, wo_ref, post_ref,
                kvn_ref, wkt_ref, wv_ref, wf_ref, wft_ref, fb_ref, fbr_ref, kkn_ref,
                fpre_ref, wq_ref, wz_ref, qn_ref, lt_ref, ut_ref, selq_ref, qc_ref,
                h1_ref, kt_ref, v_ref, q_ref, z2_ref,
                carry_scr, carry_r_scr, *, tm):
    @pl.when(pl.program_id(1) == 0)
    def _():
        carry_scr[...] = jnp.zeros_like(carry_scr)
        carry_r_scr[...] = jnp.zeros_like(carry_r_scr)

    n_split = tm // SLAB
    gated = []
    for r in range(n_split):
        rows = slice(r * SLAB, (r + 1) * SLAB)
        parts = []
        for h in range(N_HEADS):
            on = _head_rms(o_ref[0, h, rows, :].astype(F32), onorm_ref[...])
            zh = z_ref[0, rows, h * HEAD_DIM:(h + 1) * HEAD_DIM].astype(F32)
            parts.append((on * (zh * _sigmoid(zh))).astype(BF16))
        gated.append(jnp.concatenate(parts, axis=1))
    y = [_dot(gated[r], wo_ref[...]) for r in range(n_split)]
    h1_parts, u_parts, u2_parts = [], [], []
    for r in range(n_split):
        rows = slice(r * SLAB, (r + 1) * SLAB)
        h1_r = x_ref[0, rows, :] + _rms(y[r], post_ref[...])
        h1_ref[0, rows, :] = h1_r
        u_parts.append(_rms(h1_r, kvn_ref[...]).astype(BF16))
        u2_parts.append(_rms(h1_r, fpre_ref[...]).astype(BF16))

    u = jnp.concatenate(u_parts, axis=0)
    u2 = jnp.concatenate(u2_parts, axis=0)
    f = _dot(u, wf_ref[...]) + fb_ref[...]
    ft = _dot_nt(wft_ref[...], u) + fbr_ref[...]
    kt = _dot_nt(wkt_ref[...], u)
    vv = _dot(u, wv_ref[...])
    for h in range(N_HEADS):
        v_ref[0, h] = vv[:, h * HEAD_DIM:(h + 1) * HEAD_DIM].astype(BF16)

    lane = lax.broadcasted_iota(jnp.int32, f.shape, 1)
    logf = jnp.where(lane < N_HEADS, -_softplus(-f), 0.0)
    c = _dot3(lt_ref[...], logf) + carry_scr[0:1, :]
    carry_scr[...] = jnp.broadcast_to(c[tm - 1:tm, :], carry_scr.shape)
    c_hi, c_mid, c_lo = _split3(c * LOG2E)
    packed = (c_hi.astype(F32) + pltpu.roll(c_mid.astype(F32), N_HEADS, axis=1)
              + pltpu.roll(c_lo.astype(F32), 2 * N_HEADS, axis=1)).astype(BF16)
    qaug = _dot(packed, selq_ref[...]) + qc_ref[...]

    row = lax.broadcasted_iota(jnp.int32, ft.shape, 0)
    logf_r = jnp.where(row < N_HEADS, -_softplus(-ft), 0.0)
    c_r = _dot3_r(logf_r, ut_ref[...]) + carry_r_scr[...]
    carry_r_scr[...] = jnp.broadcast_to(c_r[:, tm - 1:tm], carry_r_scr.shape)
    r_hi, r_mid, r_lo = _split3(c_r * LOG2E)
    r_hi, r_mid, r_lo = r_hi.astype(F32), r_mid.astype(F32), r_lo.astype(F32)
    qq = _dot(u2, wq_ref[...])
    zz = _dot(u2, wz_ref[...])

    sub = lax.broadcasted_iota(jnp.int32, (16, tm), 0)
    for h in range(N_HEADS):
        kh = kt[h * HEAD_DIM:(h + 1) * HEAD_DIM, :]
        ms = jnp.mean(kh * kh, axis=0, keepdims=True)
        khn = (kh * lax.rsqrt(ms + EPS) * kkn_ref[...]).astype(BF16)
        aug = jnp.where(sub < 3, 1.0,
                        jnp.where(sub == 3, -r_hi[h:h + 1],
                                  jnp.where(sub == 4, -r_mid[h:h + 1],
                                            jnp.where(sub == 5, -r_lo[h:h + 1], 0.0)))).astype(BF16)
        for sl in range(tm // SLAB):
            cols = slice(sl * SLAB, (sl + 1) * SLAB)
            kt_ref[0, h, sl, 0:HEAD_DIM, :] = khn[:, cols]
            kt_ref[0, h, sl, HEAD_DIM:HEAD_DIM + 16, :] = aug[:, cols]
            kt_ref[0, h, sl, HEAD_DIM + 16:2 * HEAD_DIM, :] = jnp.zeros((HEAD_DIM - 16, SLAB), BF16)

    q_scale = HEAD_DIM ** -0.5 * LOG2E
    for h in range(N_HEADS):
        qh = _head_rms(qq[:, h * HEAD_DIM:(h + 1) * HEAD_DIM], qn_ref[...])
        q_ref[0, h, :, 0:HEAD_DIM] = (qh * q_scale).astype(BF16)
        q_ref[0, h, :, HEAD_DIM:2 * HEAD_DIM] = qaug[:, h * HEAD_DIM:(h + 1) * HEAD_DIM].astype(BF16)
    z2_ref[0] = zz.astype(BF16)


def _aug_selectors():
    selq = np.zeros((LANES, N_HEADS * HEAD_DIM), np.float32)
    qc = np.zeros((1, N_HEADS * HEAD_DIM), np.float32)
    for h in range(N_HEADS):
        for t in range(3):
            selq[t * N_HEADS + h, h * HEAD_DIM + t] = 1.0
            qc[0, h * HEAD_DIM + 3 + t] = 1.0
    return selq, qc


def _mid(o, z, x, o_norm, w_out, post_gain, kv_norm, kv_w, kv_fbias, kv_k_norm,
         fox_pre, fox_w_in, fox_q_norm, *, tm=ROW_TILE):
    B, L, D = x.shape
    H, W = N_HEADS, N_HEADS * HEAD_DIM
    assert tm % SLAB == 0
    wo = w_out.astype(BF16)
    wkt = kv_w[:, :W].T.astype(BF16)
    wv = kv_w[:, W:2 * W].astype(BF16)
    wf = jnp.zeros((D, LANES), F32).at[:, :H].set(kv_w[:, 2 * W:2 * W + H]).astype(BF16)
    wft = jnp.zeros((2 * H, D), F32).at[:H, :].set(kv_w[:, 2 * W:2 * W + H].T).astype(BF16)
    fb = jnp.zeros((1, LANES), F32).at[0, :H].set(kv_fbias.astype(F32))
    fbr = jnp.zeros((2 * H, tm), F32).at[:H, :].set(
        jnp.broadcast_to(kv_fbias.astype(F32)[:, None], (H, tm)))
    kkn = jnp.broadcast_to(kv_k_norm.astype(F32)[:, None], (HEAD_DIM, tm))
    wq = fox_w_in[:, :W].astype(BF16)
    wz = fox_w_in[:, W:2 * W].astype(BF16)
    tri = np.tril(np.ones((tm, tm), np.float32))
    selq, qc = _aug_selectors()
    row = lambda n: pl.BlockSpec((1, tm, n), lambda b, i: (b, i, 0))
    head_spec = lambda n: pl.BlockSpec((1, H, tm, n), lambda b, i: (b, 0, i, 0))
    vec = lambda a, n: a.reshape(1, n).astype(F32)
    out_shape = (
        jax.ShapeDtypeStruct((B, L, D), F32),
        jax.ShapeDtypeStruct((B, H, L // SLAB, 2 * HEAD_DIM, SLAB), BF16),
        jax.ShapeDtypeStruct((B, H, L, HEAD_DIM), BF16),
        jax.ShapeDtypeStruct((B, H, L, 2 * HEAD_DIM), BF16),
        jax.ShapeDtypeStruct((B, L, W), BF16),
    )
    return pl.pallas_call(
        functools.partial(_mid_kernel, tm=tm),
        grid=(B, L // tm),
        in_specs=[
            head_spec(HEAD_DIM), row(W), row(D),
            _resident((1, HEAD_DIM)), _resident((W, D)), _resident((1, D)),
            _resident((1, D)), _resident((W, D)), _resident((D, W)), _resident((D, LANES)),
            _resident((2 * H, D)), _resident((1, LANES)), _resident((2 * H, tm)),
            _resident((HEAD_DIM, tm)),
            _resident((1, D)), _resident((D, W)), _resident((D, W)), _resident((1, HEAD_DIM)),
            _resident((tm, tm)), _resident((tm, tm)), _resident((LANES, W)), _resident((1, W)),
        ],
        out_specs=(row(D),
                   pl.BlockSpec((1, H, tm // SLAB, 2 * HEAD_DIM, SLAB), lambda b, i: (b, 0, i, 0, 0)),
                   head_spec(HEAD_DIM), head_spec(2 * HEAD_DIM), row(W)),
        out_shape=out_shape,
        scratch_shapes=[pltpu.VMEM((8, LANES), F32), pltpu.VMEM((2 * H, tm), F32)],
        compiler_params=pltpu.CompilerParams(
            dimension_semantics=("arbitrary", "arbitrary"),
            vmem_limit_bytes=VMEM_LIMIT_BYTES),
        name="mid",
    )(o, z, x, vec(o_norm, HEAD_DIM), wo, vec(post_gain, D),
      vec(kv_norm, D), wkt, wv, wf, wft, fb, fbr, kkn,
      vec(fox_pre, D), wq, wz, vec(fox_q_norm, HEAD_DIM),
      jnp.asarray(tri, BF16), jnp.asarray(tri.T, BF16), jnp.asarray(selq, BF16), jnp.asarray(qc))


def _fox_attn_kernel(q_ref, kt_ref, v_ref, o_ref, acc_scr, m_scr, *, tq, hp):
    ts = SLAB
    i = pl.program_id(2)
    n_diag = tq // ts
    lane = lax.broadcasted_iota(jnp.int32, (ts, HEAD_DIM), 1)
    ones_col = jnp.where(lane == 0, 1.0, 0.0).astype(BF16)
    col = lax.broadcasted_iota(jnp.int32, (2 * ts, ts), 1)
    rowi = lax.broadcasted_iota(jnp.int32, (2 * ts, ts), 0)
    pair_masks = (col <= rowi, col + ts <= rowi)

    def v_aug(h, j):
        r0 = pl.multiple_of(j * ts, ts)
        return jnp.concatenate([v_ref[0, h, pl.ds(r0, ts), :], ones_col], axis=1)

    def scores(h, slabs, row0):
        q = q_ref[0, h, row0:tq, :]
        return [_dot(q, kt_ref[0, h, j]) for j in slabs]

    def finish(h, parts, slabs, row0, diag):
        if diag:
            parts = [jnp.concatenate([jnp.where(mk, p[:2 * ts], NEG_BIG), p[2 * ts:]], axis=0)
                     if p.shape[0] > 2 * ts else jnp.where(mk, p, NEG_BIG)
                     for p, mk in zip(parts, pair_masks)]
        mx = parts[0]
        for p in parts[1:]:
            mx = jnp.maximum(mx, p)
        m_old = m_scr[h, row0:tq, :]
        m_new = jnp.maximum(m_old, jnp.max(mx, axis=1, keepdims=True))
        m_scr[h, row0:tq, :] = m_new
        alpha = jnp.exp2(m_old - m_new)
        m_wide = jnp.concatenate([m_new, m_new], axis=1)
        pv = None
        for p, j in zip(parts, slabs):
            d = _dot(jnp.exp2(p - m_wide).astype(BF16), v_aug(h, j))
            pv = d if pv is None else pv + d
        acc_scr[h, row0:tq, :] = jnp.concatenate([alpha, alpha], axis=1) * acc_scr[h, row0:tq, :] + pv

    acc_scr[...] = jnp.zeros_like(acc_scr)
    m_scr[...] = jnp.full(m_scr.shape, NEG_BIG, F32)

    def step(slabs, row0, diag):
        parts = [scores(h, slabs, row0) for h in range(hp)]
        for h in range(hp):
            finish(h, parts[h], slabs, row0, diag)

    def full_blocks(jj, carry):
        step((2 * jj, 2 * jj + 1), 0, False)
        return carry

    lax.fori_loop(0, i * (n_diag // 2), full_blocks, 0)
    for d in range(n_diag // 2):
        step((i * n_diag + 2 * d, i * n_diag + 2 * d + 1), 2 * d * ts, True)
    for h in range(hp):
        acc = acc_scr[h]
        o_ref[0, h] = (acc[:, :HEAD_DIM] / acc[:, HEAD_DIM:HEAD_DIM + 1]).astype(o_ref.dtype)


def _fox_attn(q2, kt2, v, *, tq=1024, hp=4):
    B, H, L, Dh = v.shape
    tq = min(tq, L)
    assert tq % (2 * SLAB) == 0 and L % tq == 0 and H % hp == 0
    return pl.pallas_call(
        functools.partial(_fox_attn_kernel, tq=tq, hp=hp),
        grid=(B, H // hp, L // tq),
        in_specs=[pl.BlockSpec((1, hp, tq, 2 * Dh), lambda b, h, i: (b, h, i, 0)),
                  pl.BlockSpec((1, hp, L // SLAB, 2 * Dh, SLAB), lambda b, h, i: (b, h, 0, 0, 0)),
                  pl.BlockSpec((1, hp, L, Dh), lambda b, h, i: (b, h, 0, 0))],
        out_specs=pl.BlockSpec((1, hp, tq, Dh), lambda b, h, i: (b, h, i, 0)),
        out_shape=jax.ShapeDtypeStruct((B, H, L, Dh), BF16),
        scratch_shapes=[pltpu.VMEM((hp, tq, 2 * Dh), F32), pltpu.VMEM((hp, tq, LANES), F32)],
        compiler_params=pltpu.CompilerParams(
            dimension_semantics=("arbitrary", "arbitrary", "arbitrary"),
            vmem_limit_bytes=VMEM_LIMIT_BYTES),
        name="fox_attn",
    )(q2, kt2, v)


def _fox_out_kernel(o_ref, z_ref, h_ref, wo_ref, post_ref, out_ref, *, tm, n_split):
    rb = tm // n_split
    gated = []
    for r in range(n_split):
        rows = slice(r * rb, (r + 1) * rb)
        parts = []
        for h in range(N_HEADS):
            zh = z_ref[0, rows, h * HEAD_DIM:(h + 1) * HEAD_DIM].astype(F32)
            parts.append((o_ref[0, h, rows, :].astype(F32) * (zh * _sigmoid(zh))).astype(BF16))
        gated.append(jnp.concatenate(parts, axis=1))
    y = [_dot(gated[r], wo_ref[...]) for r in range(n_split)]
    for r in range(n_split):
        rows = slice(r * rb, (r + 1) * rb)
        out_ref[0, rows, :] = h_ref[0, rows, :] + _rms(y[r], post_ref[...])


def _fox_out(o, z2, h1, w_out, post_gain, *, tm=ROW_TILE):
    B, L, D = h1.shape
    H, W = N_HEADS, N_HEADS * HEAD_DIM
    row = lambda n: pl.BlockSpec((1, tm, n), lambda b, i: (b, i, 0))
    return pl.pallas_call(
        functools.partial(_fox_out_kernel, tm=tm, n_split=1),
        grid=(B, L // tm),
        in_specs=[pl.BlockSpec((1, H, tm, HEAD_DIM), lambda b, i: (b, 0, i, 0)), row(W), row(D),
                  _resident((W, D)), _resident((1, D))],
        out_specs=row(D),
        out_shape=jax.ShapeDtypeStruct((B, L, D), F32),
        compiler_params=pltpu.CompilerParams(
            dimension_semantics=("arbitrary", "arbitrary"),
            vmem_limit_bytes=VMEM_LIMIT_BYTES),
        name="fox_out",
    )(o, z2, h1, w_out.astype(BF16), post_gain.reshape(1, D).astype(F32))


def kernel(x, gdn_pre_norm, gdn_w_in, gdn_conv_w, gdn_a_log, gdn_dt_bias, gdn_o_norm, gdn_w_out, gdn_post_norm, kv_norm, kv_w, kv_forget_bias, kv_k_norm, fox_pre_norm, fox_w_in, fox_q_norm, fox_w_out, fox_post_norm):
    assert gdn_w_in.shape[0] == 1 and fox_w_in.shape[0] == 1, "one GDN layer then one FoX layer"
    q, k, v, z, gcol, grow = _gdn_in(x, gdn_pre_norm[0], gdn_w_in[0], gdn_conv_w[0],
                                     gdn_a_log[0], gdn_dt_bias[0])
    o = _gdn_scan(q, k, v, gcol, grow)
    h1, kt2, vs, q2, z2 = _mid(
        o, z, x, gdn_o_norm[0], gdn_w_out[0], gdn_post_norm[0],
        kv_norm, kv_w, kv_forget_bias, kv_k_norm, fox_pre_norm[0], fox_w_in[0], fox_q_norm[0])
    o2 = _fox_attn(q2, kt2, vs)
    return _fox_out(o2, z2, h1, fox_w_out[0], fox_post_norm[0])
```

```python
import functools
import math

import jax
import jax.numpy as jnp
import numpy as np
from jax import lax
from jax.experimental import pallas as pl
from jax.experimental.pallas import tpu as pltpu

HEAD_DIM = 128
N_HEADS = 8
CONV_K = 4
CHUNK = 64
GROUP = 1024
IN_TILE = 256
PAIR = 128
SCAN_HP = 8
SLAB = 256
ROW_TILE = 512
LOG2E = math.log2(math.e)
EPS = 1e-6
LANES = 128
VMEM_LIMIT_BYTES = 56 * 1024 * 1024
NEG_BIG = -1e30

F32 = jnp.float32
BF16 = jnp.bfloat16


def _dot(a, b):
    return jnp.dot(a, b, preferred_element_type=F32)


def _dot_nt(a, b):
    return lax.dot_general(a, b, (((1,), (1,)), ((), ())), preferred_element_type=F32)


def _dot_tn(a, b):
    return lax.dot_general(a, b, (((0,), (0,)), ((), ())), preferred_element_type=F32)


def _split3(x):
    hi = x.astype(BF16)
    r1 = x - hi.astype(F32)
    mid = r1.astype(BF16)
    lo = (r1 - mid.astype(F32)).astype(BF16)
    return hi, mid, lo


def _dot3(a_bf16, x):
    hi, mid, lo = _split3(x)
    return _dot(a_bf16, hi) + _dot(a_bf16, mid) + _dot(a_bf16, lo)


def _dot3_r(x, b_bf16):
    hi, mid, lo = _split3(x)
    return _dot(hi, b_bf16) + _dot(mid, b_bf16) + _dot(lo, b_bf16)


def _sigmoid(x):
    return 1.0 / (1.0 + jnp.exp(-x))


def _softplus(x):
    return jnp.maximum(x, 0.0) + jnp.log(1.0 + jnp.exp(-jnp.abs(x)))


def _silu(x):
    return x * _sigmoid(x)


def _l2(x):
    return x * lax.rsqrt(jnp.sum(x * x, axis=-1, keepdims=True) + EPS)


def _rms(x, gain):
    ms = jnp.mean(x * x, axis=-1, keepdims=True)
    return x * lax.rsqrt(ms + EPS) * gain


def _resident(shape):
    nd = len(shape)
    return pl.BlockSpec(shape, lambda *_: (0,) * nd, pipeline_mode=pl.Buffered(1))


def _gdn_in_kernel(x_ref, gain_ref, w_ref, wab_ref, wabt_ref, convw_ref, hpc_ref, hpr_ref,
                   lt_ref, ut_ref,
                   q_ref, k_ref, v_ref, z_ref, gcol_ref, grow_ref,
                   xn_scr, pre_scr, carry_scr, *, tm, width, col_chunk):
    @pl.when(pl.program_id(1) == 0)
    def _():
        carry_scr[...] = jnp.zeros_like(carry_scr)

    xn_scr[...] = _rms(x_ref[0], gain_ref[...]).astype(BF16)
    xn = xn_scr[...]

    def z_chunk(c):
        lo = 3 * width + c * col_chunk
        z_ref[0, :, c * col_chunk:(c + 1) * col_chunk] = _dot(xn, w_ref[:, lo:lo + col_chunk]).astype(BF16)

    n_qkv = 3 * width // col_chunk
    n_z = width // col_chunk
    z_after = {(i + 1) * n_qkv // (n_z + 1) - 1: i for i in range(n_z)}

    def conv_chunk(c):
        lo = c * col_chunk
        pc = _dot(xn, w_ref[:, lo:lo + col_chunk])
        pre_scr[0:8, :] = carry_scr[:, lo:lo + col_chunk]
        pre_scr[8:8 + tm, :] = pc
        carry_scr[:, lo:lo + col_chunk] = pc[tm - 8:tm, :]
        cw = convw_ref[:, lo:lo + col_chunk]
        acc = pc * cw[CONV_K - 1:CONV_K, :]
        for j in range(1, CONV_K):
            acc = acc + pre_scr[8 - j:8 - j + tm, :] * cw[CONV_K - 1 - j:CONV_K - j, :]
        for hh in range(col_chunk // HEAD_DIM):
            col = lo + hh * HEAD_DIM
            out_ref = (q_ref, k_ref, v_ref)[col // width]
            out_ref[0, (col % width) // HEAD_DIM] = acc[:, hh * HEAD_DIM:(hh + 1) * HEAD_DIM]
        if c in z_after:
            z_chunk(z_after[c])

    for c in range(n_qkv // 2):
        conv_chunk(c)

    ab = _dot(xn, wab_ref[...])
    lane = lax.broadcasted_iota(jnp.int32, ab.shape, 1)
    g_col = jnp.where(lane < N_HEADS, hpc_ref[0:1, :] * _softplus(ab + hpc_ref[1:2, :]), 0.0)
    beta_col = _sigmoid(ab)
    cums = _dot3(lt_ref[...], g_col)
    gc_col = cums[0:tm]
    gl_col = pltpu.roll(cums[tm:2 * tm], 2 * N_HEADS, axis=1)
    gates = jnp.where(lane < N_HEADS, gc_col, jnp.where(lane < 2 * N_HEADS, beta_col, gl_col))
    for hg in range(N_HEADS // SCAN_HP):
        shift = (LANES - hg * SCAN_HP) % LANES
        gcol_ref[0, hg] = gates if shift == 0 else pltpu.roll(gates, shift, axis=1)

    abt = _dot_nt(wabt_ref[...], xn)
    row = lax.broadcasted_iota(jnp.int32, abt.shape, 0)
    g_row = jnp.where(row < N_HEADS, hpr_ref[0] * _softplus(abt + hpr_ref[1]), 0.0)
    gc_row = _dot3_r(g_row, ut_ref[...])
    for hg in range(N_HEADS // SCAN_HP):
        grow_ref[0, hg] = gc_row[hg * SCAN_HP:hg * SCAN_HP + 8, :]

    for c in range(n_qkv // 2, n_qkv):
        conv_chunk(c)


def _chunk_tri(tm):
    t = np.arange(tm)
    same = (t[:, None] // CHUNK) == (t[None, :] // CHUNK)
    lower = same & (t[None, :] <= t[:, None])
    return lower.astype(np.float32), same.astype(np.float32)


def _gdn_in(x, pre_gain, w_in, conv_w, a_log, dt_bias, *, tm=IN_TILE, col_chunk=256):
    B, L, D = x.shape
    W = N_HEADS * HEAD_DIM
    H = N_HEADS
    w_main = w_in[:, :4 * W].astype(BF16)
    wab = jnp.zeros((D, LANES), F32).at[:, :2 * H].set(w_in[:, 4 * W:4 * W + 2 * H]).astype(BF16)
    wabt = w_in[:, 4 * W:4 * W + 2 * H].T.astype(BF16)
    neg_a = -jnp.exp(a_log.astype(F32))
    hpc = jnp.zeros((2, LANES), F32).at[0, :H].set(neg_a).at[1, :H].set(dt_bias.astype(F32))
    hpr = jnp.zeros((2, 2 * H, tm), F32)
    hpr = hpr.at[0, :H, :].set(jnp.broadcast_to(neg_a[:, None], (H, tm)))
    hpr = hpr.at[1, :H, :].set(jnp.broadcast_to(dt_bias.astype(F32)[:, None], (H, tm)))
    lower, same = _chunk_tri(tm)
    lt = jnp.asarray(np.concatenate([lower, same], axis=0), BF16)
    ut = jnp.asarray(lower.T, BF16)

    kern = functools.partial(_gdn_in_kernel, tm=tm, width=W, col_chunk=col_chunk)
    row_spec = lambda n: pl.BlockSpec((1, tm, n), lambda b, i: (b, i, 0))
    head_spec = pl.BlockSpec((1, H, tm, HEAD_DIM), lambda b, i: (b, 0, i, 0))
    out_shape = (
        jax.ShapeDtypeStruct((B, H, L, HEAD_DIM), F32),
        jax.ShapeDtypeStruct((B, H, L, HEAD_DIM), F32),
        jax.ShapeDtypeStruct((B, H, L, HEAD_DIM), F32),
        jax.ShapeDtypeStruct((B, L, W), BF16),
        jax.ShapeDtypeStruct((B, H // SCAN_HP, L, LANES), F32),
        jax.ShapeDtypeStruct((B, H // SCAN_HP, 8, L), F32),
    )
    return pl.pallas_call(
        kern,
        grid=(B, L // tm),
        in_specs=[
            row_spec(D),
            _resident((1, D)),
            _resident((D, 4 * W)),
            _resident((D, LANES)),
            _resident((2 * H, D)),
            _resident((CONV_K, 3 * W)),
            _resident((2, LANES)),
            _resident((2, 2 * H, tm)),
            _resident((2 * tm, tm)),
            _resident((tm, tm)),
        ],
        out_specs=(head_spec, head_spec, head_spec, row_spec(W),
                   pl.BlockSpec((1, H // SCAN_HP, tm, LANES), lambda b, i: (b, 0, i, 0)),
                   pl.BlockSpec((1, H // SCAN_HP, 8, tm), lambda b, i: (b, 0, 0, i))),
        out_shape=out_shape,
        scratch_shapes=[
            pltpu.VMEM((tm, D), BF16),
            pltpu.VMEM((tm + 8, col_chunk), F32),
            pltpu.VMEM((8, 3 * W), F32),
        ],
        compiler_params=pltpu.CompilerParams(
            dimension_semantics=("arbitrary", "arbitrary"),
            vmem_limit_bytes=VMEM_LIMIT_BYTES),
        name="gdn_in",
    )(x, pre_gain.reshape(1, D).astype(F32), w_main, wab, wabt, conv_w.astype(F32), hpc, hpr, lt, ut)


def _gdn_scan_kernel(q_ref, k_ref, v_ref, gcol_ref, grow_ref, cmask_ref, smask_ref, eye_ref,
                     o_ref, s_scr, *, hp):
    P = PAIR
    n_pair = GROUP // P
    cpp = P // CHUNK

    @pl.when(pl.program_id(2) == 0)
    def _():
        s_scr[...] = jnp.zeros_like(s_scr)

    gcol = gcol_ref[0, 0]
    grow = grow_ref[0, 0]
    cmask = cmask_ref[...] > 0
    smask = smask_ref[...]
    eye = eye_ref[...]

    heads = range(hp)

    def setup(j):
        rows = slice(j * P, (j + 1) * P)
        st = dict(glb=[], kbg=[], vb=[], qg=[], kdec=[], attn=[], x=[], p=[])
        for h in heads:
            qf = _l2(_silu(q_ref[0, h, rows, :])) * (HEAD_DIM ** -0.5)
            kf = _l2(_silu(k_ref[0, h, rows, :]))
            vf = _silu(v_ref[0, h, rows, :])
            qh, kh = qf.astype(BF16), kf.astype(BF16)
            gc = jnp.broadcast_to(gcol[rows, h:h + 1], (P, LANES))
            beta = jnp.broadcast_to(gcol[rows, N_HEADS + h:N_HEADS + h + 1], (P, LANES))
            gl = jnp.broadcast_to(gcol[rows, 2 * N_HEADS + h:2 * N_HEADS + h + 1], (P, LANES))
            kb = kf * beta
            egc = jnp.exp(gc)
            st["glb"].append(gl)
            st["kbg"].append((kb * egc).astype(BF16))
            st["vb"].append((vf * beta).astype(BF16))
            st["qg"].append((qf * egc).astype(BF16))
            st["kdec"].append((kf * jnp.exp(gl - gc)).astype(BF16))
            kk = _dot_nt(kb.astype(BF16), kh)
            qk = _dot_nt(qh, kh)
            diff = gc - grow[h:h + 1, rows]
            dec = jnp.exp(jnp.where(cmask, diff, NEG_BIG))
            st["attn"].append((qk * dec).astype(BF16))
            st["x"].append(-(kk * dec * smask))
            st["p"].append(eye + st["x"][-1])
        return st

    def invert(st):
        x, p = st["x"], st["p"]
        xb = [x[h].astype(BF16) for h in heads]
        x = [_dot(xb[h], xb[h]) for h in heads]
        yield
        for _ in range(4):
            xb = [x[h].astype(BF16) for h in heads]
            r = [_dot(jnp.concatenate([xb[h], p[h].astype(BF16)], axis=0), xb[h]) for h in heads]
            x = [r[h][:P] for h in heads]
            p = [p[h] + r[h][P:] for h in heads]
            yield
        p = [p[h] + _dot(p[h].astype(BF16), x[h].astype(BF16)) for h in heads]
        yield
        uw = [_dot(p[h].astype(BF16), jnp.concatenate([st["vb"][h], st["kbg"][h]], axis=1)) for h in heads]
        st["u"] = [uw[h][:, :HEAD_DIM] for h in heads]
        st["wm"] = [uw[h][:, HEAD_DIM:].astype(BF16) for h in heads]
        yield

    def advance(st, j, s):
        vn_parts = [[] for _ in heads]
        for c in range(cpp):
            r0 = c * CHUNK
            wq = [_dot(jnp.concatenate([st["wm"][h][r0:r0 + CHUNK], st["qg"][h][r0:r0 + CHUNK]], axis=0),
                       s[h].astype(BF16)) for h in heads]
            yield
            vn = [(st["u"][h][r0:r0 + CHUNK] - wq[h][:CHUNK]).astype(BF16) for h in heads]
            pad = [jnp.zeros(((cpp - 1 - c) * CHUNK, HEAD_DIM), BF16)] if c < cpp - 1 else []
            for h in heads:
                vn_parts[h].append(vn[h])
                vn_all = jnp.concatenate(vn_parts[h] + pad, axis=0)
                o_c = wq[h][CHUNK:] + _dot(st["attn"][h][r0:r0 + CHUNK, :], vn_all)
                o_ref[0, h, j * P + r0:j * P + r0 + CHUNK, :] = o_c.astype(o_ref.dtype)
            yield
            for h in heads:
                s[h] = (s[h] * jnp.exp(st["glb"][h][r0:r0 + 1, :])
                        + _dot_tn(st["kdec"][h][r0:r0 + CHUNK], vn[h]))
            yield

    s = [s_scr[h] for h in heads]
    prev = None
    for j in range(n_pair):
        st = setup(j)
        gens = [invert(st)] + ([advance(prev, j - 1, s)] if prev is not None else [])
        while gens:
            for g in list(gens):
                if next(g, "done") == "done":
                    gens.remove(g)
        prev = st
    for _ in advance(prev, n_pair - 1, s):
        pass
    for h in heads:
        s_scr[h] = s[h]


def _pair_masks():
    t = np.arange(PAIR)
    same = (t[:, None] // CHUNK) == (t[None, :] // CHUNK)
    causal = same & (t[None, :] <= t[:, None])
    strict = same & (t[None, :] < t[:, None])
    return causal.astype(np.float32), strict.astype(np.float32), np.eye(PAIR, dtype=np.float32)


def _gdn_scan(q, k, v, gcol, grow):
    B, H, L, Dh = q.shape
    hp = SCAN_HP
    causal, strict, eye = _pair_masks()
    head_spec = pl.BlockSpec((1, hp, GROUP, Dh), lambda b, hg, g: (b, hg, g, 0))
    return pl.pallas_call(
        functools.partial(_gdn_scan_kernel, hp=hp),
        grid=(B, H // hp, L // GROUP),
        in_specs=[
            head_spec, head_spec, head_spec,
            pl.BlockSpec((1, 1, GROUP, LANES), lambda b, hg, g: (b, hg, g, 0)),
            pl.BlockSpec((1, 1, 8, GROUP), lambda b, hg, g: (b, hg, 0, g)),
            _resident((PAIR, PAIR)), _resident((PAIR, PAIR)), _resident((PAIR, PAIR)),
        ],
        out_specs=head_spec,
        out_shape=jax.ShapeDtypeStruct((B, H, L, Dh), BF16),
        scratch_shapes=[pltpu.VMEM((hp, Dh, Dh), F32)],
        compiler_params=pltpu.CompilerParams(
            dimension_semantics=("arbitrary", "arbitrary", "arbitrary"),
            vmem_limit_bytes=VMEM_LIMIT_BYTES),
        name="gdn_scan",
    )(q, k, v, gcol, grow, jnp.asarray(causal), jnp.asarray(strict), jnp.asarray(eye))


def _head_rms(xh, gain):
    ms = jnp.mean(xh * xh, axis=-1, keepdims=True)
    return xh * lax.rsqrt(ms + EPS) * gain


def _mid_kernel(o_ref, z_ref, x_ref, onorm_ref, wo_ref, post_ref,
                kvn_ref, wkt_ref, wv_ref, wf_ref, wft_ref, fb_ref, fbr_ref, kkn_ref,
                fpre_ref, wq_ref, wz_ref, qn_ref, lt_ref, ut_ref, selq_ref, qc_ref,
                h1_ref, kt_ref, v_ref, q_ref, z2_ref,
                carry_scr, carry_r_scr, *, tm):
    @pl.when(pl.program_id(1) == 0)
    def _():
        carry_scr[...] = jnp.zeros_like(carry_scr)
        carry_r_scr[...] = jnp.zeros_like(carry_r_scr)

    n_split = tm // SLAB
    gated = []
    for r in range(n_split):
        rows = slice(r * SLAB, (r + 1) * SLAB)
        parts = []
        for h in range(N_HEADS):
            on = _head_rms(o_ref[0, h, rows, :].astype(F32), onorm_ref[...])
            zh = z_ref[0, rows, h * HEAD_DIM:(h + 1) * HEAD_DIM].astype(F32)
            parts.append((on * (zh * _sigmoid(zh))).astype(BF16))
        gated.append(jnp.concatenate(parts, axis=1))
    y = [_dot(gated[r], wo_ref[...]) for r in range(n_split)]
    h1_parts, u_parts, u2_parts = [], [], []
    for r in range(n_split):
        rows = slice(r * SLAB, (r + 1) * SLAB)
        h1_r = x_ref[0, rows, :] + _rms(y[r], post_ref[...])
        h1_ref[0, rows, :] = h1_r
        u_parts.append(_rms(h1_r, kvn_ref[...]).astype(BF16))
        u2_parts.append(_rms(h1_r, fpre_ref[...]).astype(BF16))

    u = jnp.concatenate(u_parts, axis=0)
    u2 = jnp.concatenate(u2_parts, axis=0)
    f = _dot(u, wf_ref[...]) + fb_ref[...]
    ft = _dot_nt(wft_ref[...], u) + fbr_ref[...]
    kt = _dot_nt(wkt_ref[...], u)
    vv = _dot(u, wv_ref[...])
    for h in range(N_HEADS):
        v_ref[0, h] = vv[:, h * HEAD_DIM:(h + 1) * HEAD_DIM].astype(BF16)

    lane = lax.broadcasted_iota(jnp.int32, f.shape, 1)
    logf = jnp.where(lane < N_HEADS, -_softplus(-f), 0.0)
    c = _dot3(lt_ref[...], logf) + carry_scr[0:1, :]
    carry_scr[...] = jnp.broadcast_to(c[tm - 1:tm, :], carry_scr.shape)
    c_hi, c_mid, c_lo = _split3(c * LOG2E)
    packed = (c_hi.astype(F32) + pltpu.roll(c_mid.astype(F32), N_HEADS, axis=1)
              + pltpu.roll(c_lo.astype(F32), 2 * N_HEADS, axis=1)).astype(BF16)
    qaug = _dot(packed, selq_ref[...]) + qc_ref[...]

    row = lax.broadcasted_iota(jnp.int32, ft.shape, 0)
    logf_r = jnp.where(row < N_HEADS, -_softplus(-ft), 0.0)
    c_r = _dot3_r(logf_r, ut_ref[...]) + carry_r_scr[...]
    carry_r_scr[...] = jnp.broadcast_to(c_r[:, tm - 1:tm], carry_r_scr.shape)
    r_hi, r_mid, r_lo = _split3(c_r * LOG2E)
    r_hi, r_mid, r_lo = r_hi.astype(F32), r_mid.astype(F32), r_lo.astype(F32)
    qq = _dot(u2, wq_ref[...])
    zz = _dot(u2, wz_ref[...])

    sub = lax.broadcasted_iota(jnp.int32, (16, tm), 0)
    for h in range(N_HEADS):
        kh = kt[h * HEAD_DIM:(h + 1) * HEAD_DIM, :]
        ms = jnp.mean(kh * kh, axis=0, keepdims=True)
        khn = (kh * lax.rsqrt(ms + EPS) * kkn_ref[...]).astype(BF16)
        aug = jnp.where(sub < 3, 1.0,
                        jnp.where(sub == 3, -r_hi[h:h + 1],
                                  jnp.where(sub == 4, -r_mid[h:h + 1],
                                            jnp.where(sub == 5, -r_lo[h:h + 1], 0.0)))).astype(BF16)
        for sl in range(tm // SLAB):
            cols = slice(sl * SLAB, (sl + 1) * SLAB)
            kt_ref[0, h, sl, 0:HEAD_DIM, :] = khn[:, cols]
            kt_ref[0, h, sl, HEAD_DIM:HEAD_DIM + 16, :] = aug[:, cols]
            kt_ref[0, h, sl, HEAD_DIM + 16:2 * HEAD_DIM, :] = jnp.zeros((HEAD_DIM - 16, SLAB), BF16)

    q_scale = HEAD_DIM ** -0.5 * LOG2E
    for h in range(N_HEADS):
        qh = _head_rms(qq[:, h * HEAD_DIM:(h + 1) * HEAD_DIM], qn_ref[...])
        q_ref[0, h, :, 0:HEAD_DIM] = (qh * q_scale).astype(BF16)
        q_ref[0, h, :, HEAD_DIM:2 * HEAD_DIM] = qaug[:, h * HEAD_DIM:(h + 1) * HEAD_DIM].astype(BF16)
    z2_ref[0] = zz.astype(BF16)


def _aug_selectors():
    selq = np.zeros((LANES, N_HEADS * HEAD_DIM), np.float32)
    qc = np.zeros((1, N_HEADS * HEAD_DIM), np.float32)
    for h in range(N_HEADS):
        for t in range(3):
            selq[t * N_HEADS + h, h * HEAD_DIM + t] = 1.0
            qc[0, h * HEAD_DIM + 3 + t] = 1.0
    return selq, qc


def _mid(o, z, x, o_norm, w_out, post_gain, kv_norm, kv_w, kv_fbias, kv_k_norm,
         fox_pre, fox_w_in, fox_q_norm, *, tm=ROW_TILE):
    B, L, D = x.shape
    H, W = N_HEADS, N_HEADS * HEAD_DIM
    assert tm % SLAB == 0
    wo = w_out.astype(BF16)
    wkt = kv_w[:, :W].T.astype(BF16)
    wv = kv_w[:, W:2 * W].astype(BF16)
    wf = jnp.zeros((D, LANES), F32).at[:, :H].set(kv_w[:, 2 * W:2 * W + H]).astype(BF16)
    wft = jnp.zeros((2 * H, D), F32).at[:H, :].set(kv_w[:, 2 * W:2 * W + H].T).astype(BF16)
    fb = jnp.zeros((1, LANES), F32).at[0, :H].set(kv_fbias.astype(F32))
    fbr = jnp.zeros((2 * H, tm), F32).at[:H, :].set(
        jnp.broadcast_to(kv_fbias.astype(F32)[:, None], (H, tm)))
    kkn = jnp.broadcast_to(kv_k_norm.astype(F32)[:, None], (HEAD_DIM, tm))
    wq = fox_w_in[:, :W].astype(BF16)
    wz = fox_w_in[:, W:2 * W].astype(BF16)
    tri = np.tril(np.ones((tm, tm), np.float32))
    selq, qc = _aug_selectors()
    row = lambda n: pl.BlockSpec((1, tm, n), lambda b, i: (b, i, 0))
    head_spec = lambda n: pl.BlockSpec((1, H, tm, n), lambda b, i: (b, 0, i, 0))
    vec = lambda a, n: a.reshape(1, n).astype(F32)
    out_shape = (
        jax.ShapeDtypeStruct((B, L, D), F32),
        jax.ShapeDtypeStruct((B, H, L // SLAB, 2 * HEAD_DIM, SLAB), BF16),
        jax.ShapeDtypeStruct((B, H, L, HEAD_DIM), BF16),
        jax.ShapeDtypeStruct((B, H, L, 2 * HEAD_DIM), BF16),
        jax.ShapeDtypeStruct((B, L, W), BF16),
    )
    return pl.pallas_call(
        functools.partial(_mid_kernel, tm=tm),
        grid=(B, L // tm),
        in_specs=[
            head_spec(HEAD_DIM), row(W), row(D),
            _resident((1, HEAD_DIM)), _resident((W, D)), _resident((1, D)),
            _resident((1, D)), _resident((W, D)), _resident((D, W)), _resident((D, LANES)),
            _resident((2 * H, D)), _resident((1, LANES)), _resident((2 * H, tm)),
            _resident((HEAD_DIM, tm)),
            _resident((1, D)), _resident((D, W)), _resident((D, W)), _resident((1, HEAD_DIM)),
            _resident((tm, tm)), _resident((tm, tm)), _resident((LANES, W)), _resident((1, W)),
        ],
        out_specs=(row(D),
                   pl.BlockSpec((1, H, tm // SLAB, 2 * HEAD_DIM, SLAB), lambda b, i: (b, 0, i, 0, 0)),
                   head_spec(HEAD_DIM), head_spec(2 * HEAD_DIM), row(W)),
        out_shape=out_shape,
        scratch_shapes=[pltpu.VMEM((8, LANES), F32), pltpu.VMEM((2 * H, tm), F32)],
        compiler_params=pltpu.CompilerParams(
            dimension_semantics=("arbitrary", "arbitrary"),
            vmem_limit_bytes=VMEM_LIMIT_BYTES),
        name="mid",
    )(o, z, x, vec(o_norm, HEAD_DIM), wo, vec(post_gain, D),
      vec(kv_norm, D), wkt, wv, wf, wft, fb, fbr, kkn,
      vec(fox_pre, D), wq, wz, vec(fox_q_norm, HEAD_DIM),
      jnp.asarray(tri, BF16), jnp.asarray(tri.T, BF16), jnp.asarray(selq, BF16), jnp.asarray(qc))


def _fox_attn_kernel(q_ref, kt_ref, v_ref, o_ref, acc_scr, m_scr, *, tq, hp):
    ts = SLAB
    i = pl.program_id(2)
    n_diag = tq // ts
    lane = lax.broadcasted_iota(jnp.int32, (ts, HEAD_DIM), 1)
    ones_col = jnp.where(lane == 0, 1.0, 0.0).astype(BF16)
    col = lax.broadcasted_iota(jnp.int32, (2 * ts, ts), 1)
    rowi = lax.broadcasted_iota(jnp.int32, (2 * ts, ts), 0)
    pair_masks = (col <= rowi, col + ts <= rowi)

    def v_aug(h, j):
        r0 = pl.multiple_of(j * ts, ts)
        return jnp.concatenate([v_ref[0, h, pl.ds(r0, ts), :], ones_col], axis=1)

    def scores(h, slabs, row0):
        q = q_ref[0, h, row0:tq, :]
        return [_dot(q, kt_ref[0, h, j]) for j in slabs]

    def finish(h, parts, slabs, row0, diag):
        if diag:
            parts = [jnp.concatenate([jnp.where(mk, p[:2 * ts], NEG_BIG), p[2 * ts:]], axis=0)
                     if p.shape[0] > 2 * ts else jnp.where(mk, p, NEG_BIG)
                     for p, mk in zip(parts, pair_masks)]
        mx = parts[0]
        for p in parts[1:]:
            mx = jnp.maximum(mx, p)
        m_old = m_scr[h, row0:tq, :]
        m_new = jnp.maximum(m_old, jnp.max(mx, axis=1, keepdims=True))
        m_scr[h, row0:tq, :] = m_new
        alpha = jnp.exp2(m_old - m_new)
        m_wide = jnp.concatenate([m_new, m_new], axis=1)
        pv = None
        for p, j in zip(parts, slabs):
            d = _dot(jnp.exp2(p - m_wide).astype(BF16), v_aug(h, j))
            pv = d if pv is None else pv + d
        acc_scr[h, row0:tq, :] = jnp.concatenate([alpha, alpha], axis=1) * acc_scr[h, row0:tq, :] + pv

    acc_scr[...] = jnp.zeros_like(acc_scr)
    m_scr[...] = jnp.full(m_scr.shape, NEG_BIG, F32)

    def step(slabs, row0, diag):
        parts = [scores(h, slabs, row0) for h in range(hp)]
        for h in range(hp):
            finish(h, parts[h], slabs, row0, diag)

    def full_blocks(jj, carry):
        step((2 * jj, 2 * jj + 1), 0, False)
        return carry

    lax.fori_loop(0, i * (n_diag // 2), full_blocks, 0)
    for d in range(n_diag // 2):
        step((i * n_diag + 2 * d, i * n_diag + 2 * d + 1), 2 * d * ts, True)
    for h in range(hp):
        acc = acc_scr[h]
        o_ref[0, h] = (acc[:, :HEAD_DIM] / acc[:, HEAD_DIM:HEAD_DIM + 1]).astype(o_ref.dtype)


def _fox_attn(q2, kt2, v, *, tq=1024, hp=4):
    B, H, L, Dh = v.shape
    tq = min(tq, L)
    assert tq % (2 * SLAB) == 0 and L % tq == 0 and H % hp == 0
    return pl.pallas_call(
        functools.partial(_fox_attn_kernel, tq=tq, hp=hp),
        grid=(B, H // hp, L // tq),
        in_specs=[pl.BlockSpec((1, hp, tq, 2 * Dh), lambda b, h, i: (b, h, i, 0)),
                  pl.BlockSpec((1, hp, L // SLAB, 2 * Dh, SLAB), lambda b, h, i: (b, h, 0, 0, 0)),
                  pl.BlockSpec((1, hp, L, Dh), lambda b, h, i: (b, h, 0, 0))],
        out_specs=pl.BlockSpec((1, hp, tq, Dh), lambda b, h, i: (b, h, i, 0)),
        out_shape=jax.ShapeDtypeStruct((B, H, L, Dh), BF16),
        scratch_shapes=[pltpu.VMEM((hp, tq, 2 * Dh), F32), pltpu.VMEM((hp, tq, LANES), F32)],
        compiler_params=pltpu.CompilerParams(
            dimension_semantics=("arbitrary", "arbitrary", "arbitrary"),
            vmem_limit_bytes=VMEM_LIMIT_BYTES),
        name="fox_attn",
    )(q2, kt2, v)


def _fox_out_kernel(o_ref, z_ref, h_ref, wo_ref, post_ref, out_ref, *, tm, n_split):
    rb = tm // n_split
    gated = []
    for r in range(n_split):
        rows = slice(r * rb, (r + 1) * rb)
        parts = []
        for h in range(N_HEADS):
            zh = z_ref[0, rows, h * HEAD_DIM:(h + 1) * HEAD_DIM].astype(F32)
            parts.append((o_ref[0, h, rows, :].astype(F32) * (zh * _sigmoid(zh))).astype(BF16))
        gated.append(jnp.concatenate(parts, axis=1))
    y = [_dot(gated[r], wo_ref[...]) for r in range(n_split)]
    for r in range(n_split):
        rows = slice(r * rb, (r + 1) * rb)
        out_ref[0, rows, :] = h_ref[0, rows, :] + _rms(y[r], post_ref[...])


def _fox_out(o, z2, h1, w_out, post_gain, *, tm=ROW_TILE):
    B, L, D = h1.shape
    H, W = N_HEADS, N_HEADS * HEAD_DIM
    row = lambda n: pl.BlockSpec((1, tm, n), lambda b, i: (b, i, 0))
    return pl.pallas_call(
        functools.partial(_fox_out_kernel, tm=tm, n_split=1),
        grid=(B, L // tm),
        in_specs=[pl.BlockSpec((1, H, tm, HEAD_DIM), lambda b, i: (b, 0, i, 0)), row(W), row(D),
                  _resident((W, D)), _resident((1, D))],
        out_specs=row(D),
        out_shape=jax.ShapeDtypeStruct((B, L, D), F32),
        compiler_params=pltpu.CompilerParams(
            dimension_semantics=("arbitrary", "arbitrary"),
            vmem_limit_bytes=VMEM_LIMIT_BYTES),
        name="fox_out",
    )(o, z2, h1, w_out.astype(BF16), post_gain.reshape(1, D).astype(F32))


def kernel(x, gdn_pre_norm, gdn_w_in, gdn_conv_w, gdn_a_log, gdn_dt_bias, gdn_o_norm, gdn_w_out, gdn_post_norm, kv_norm, kv_w, kv_forget_bias, kv_k_norm, fox_pre_norm, fox_w_in, fox_q_norm, fox_w_out, fox_post_norm):
    assert gdn_w_in.shape[0] == 1 and fox_w_in.shape[0] == 1, "one GDN layer then one FoX layer"
    q, k, v, z, gcol, grow = _gdn_in(x, gdn_pre_norm[0], gdn_w_in[0], gdn_conv_w[0],
                                     gdn_a_log[0], gdn_dt_bias[0])
    o = _gdn_scan(q, k, v, gcol, grow)
    h1, kt2, vs, q2, z2 = _mid(
        o, z, x, gdn_o_norm[0], gdn_w_out[0], gdn_post_norm[0],
        kv_norm, kv_w, kv_forget_bias, kv_k_norm, fox_pre_norm[0], fox_w_in[0], fox_q_norm[0])
    o2 = _fox_attn(q2, kt2, vs)
    return _fox_out(o2, z2, h1, fox_w_out[0], fox_post_norm[0])
```

```python
import functools
import math

import jax
import jax.numpy as jnp
import numpy as np
from jax import lax
from jax.experimental import pallas as pl
from jax.experimental.pallas import tpu as pltpu

HEAD_DIM = 128
N_HEADS = 8
CONV_K = 4
CHUNK = 64
GROUP = 1024
IN_TILE = 256
PAIR = 128
SCAN_HP = 8
SLAB = 256
ROW_TILE = 512
MID_SUB = 128
LOG2E = math.log2(math.e)
EPS = 1e-6
LANES = 128
VMEM_LIMIT_BYTES = 56 * 1024 * 1024
NEG_BIG = -1e30

F32 = jnp.float32
BF16 = jnp.bfloat16


def _dot(a, b):
    return jnp.dot(a, b, preferred_element_type=F32)


def _dot_nt(a, b):
    return lax.dot_general(a, b, (((1,), (1,)), ((), ())), preferred_element_type=F32)


def _dot_tn(a, b):
    return lax.dot_general(a, b, (((0,), (0,)), ((), ())), preferred_element_type=F32)


def _split3(x):
    hi = x.astype(BF16)
    r1 = x - hi.astype(F32)
    mid = r1.astype(BF16)
    lo = (r1 - mid.astype(F32)).astype(BF16)
    return hi, mid, lo


def _dot3(a_bf16, x):
    hi, mid, lo = _split3(x)
    return _dot(a_bf16, hi) + _dot(a_bf16, mid) + _dot(a_bf16, lo)


def _dot3_r(x, b_bf16):
    hi, mid, lo = _split3(x)
    return _dot(hi, b_bf16) + _dot(mid, b_bf16) + _dot(lo, b_bf16)


def _sigmoid(x):
    return 1.0 / (1.0 + jnp.exp(-x))


def _softplus(x):
    return jnp.maximum(x, 0.0) + jnp.log(1.0 + jnp.exp(-jnp.abs(x)))


def _silu(x):
    return x * _sigmoid(x)


def _l2(x):
    return x * lax.rsqrt(jnp.sum(x * x, axis=-1, keepdims=True) + EPS)


def _rms(x, gain):
    ms = jnp.mean(x * x, axis=-1, keepdims=True)
    return x * lax.rsqrt(ms + EPS) * gain


def _resident(shape):
    nd = len(shape)
    return pl.BlockSpec(shape, lambda *_: (0,) * nd, pipeline_mode=pl.Buffered(1))


def _gdn_in_kernel(x_ref, gain_ref, w_ref, wab_ref, wabt_ref, convw_ref, hpc_ref, hpr_ref,
                   lt_ref, ut_ref,
                   q_ref, k_ref, v_ref, z_ref, gcol_ref, grow_ref,
                   xn_scr, pre_scr, carry_scr, *, tm, width, col_chunk):
    @pl.when(pl.program_id(1) == 0)
    def _():
        carry_scr[...] = jnp.zeros_like(carry_scr)

    xn_scr[...] = _rms(x_ref[0], gain_ref[...]).astype(BF16)
    xn = xn_scr[...]

    def z_chunk(c):
        lo = 3 * width + c * col_chunk
        z_ref[0, :, c * col_chunk:(c + 1) * col_chunk] = _dot(xn, w_ref[:, lo:lo + col_chunk]).astype(BF16)

    n_qkv = 3 * width // col_chunk
    n_z = width // col_chunk
    z_after = {(i + 1) * n_qkv // (n_z + 1) - 1: i for i in range(n_z)}

    def conv_chunk(c):
        lo = c * col_chunk
        pc = _dot(xn, w_ref[:, lo:lo + col_chunk])
        pre_scr[0:8, :] = carry_scr[:, lo:lo + col_chunk]
        pre_scr[8:8 + tm, :] = pc
        carry_scr[:, lo:lo + col_chunk] = pc[tm - 8:tm, :]
        cw = convw_ref[:, lo:lo + col_chunk]
        acc = pc * cw[CONV_K - 1:CONV_K, :]
        for j in range(1, CONV_K):
            acc = acc + pre_scr[8 - j:8 - j + tm, :] * cw[CONV_K - 1 - j:CONV_K - j, :]
        for hh in range(col_chunk // HEAD_DIM):
            col = lo + hh * HEAD_DIM
            out_ref = (q_ref, k_ref, v_ref)[col // width]
            out_ref[0, (col % width) // HEAD_DIM] = acc[:, hh * HEAD_DIM:(hh + 1) * HEAD_DIM]
        if c in z_after:
            z_chunk(z_after[c])

    for c in range(n_qkv // 2):
        conv_chunk(c)

    ab = _dot(xn, wab_ref[...])
    lane = lax.broadcasted_iota(jnp.int32, ab.shape, 1)
    g_col = jnp.where(lane < N_HEADS, hpc_ref[0:1, :] * _softplus(ab + hpc_ref[1:2, :]), 0.0)
    beta_col = _sigmoid(ab)
    cums = _dot3(lt_ref[...], g_col)
    gc_col = cums[0:tm]
    gl_col = pltpu.roll(cums[tm:2 * tm], 2 * N_HEADS, axis=1)
    gates = jnp.where(lane < N_HEADS, gc_col, jnp.where(lane < 2 * N_HEADS, beta_col, gl_col))
    for hg in range(N_HEADS // SCAN_HP):
        shift = (LANES - hg * SCAN_HP) % LANES
        gcol_ref[0, hg] = gates if shift == 0 else pltpu.roll(gates, shift, axis=1)

    abt = _dot_nt(wabt_ref[...], xn)
    row = lax.broadcasted_iota(jnp.int32, abt.shape, 0)
    g_row = jnp.where(row < N_HEADS, hpr_ref[0] * _softplus(abt + hpr_ref[1]), 0.0)
    gc_row = _dot3_r(g_row, ut_ref[...])
    for hg in range(N_HEADS // SCAN_HP):
        grow_ref[0, hg] = gc_row[hg * SCAN_HP:hg * SCAN_HP + 8, :]

    for c in range(n_qkv // 2, n_qkv):
        conv_chunk(c)


def _chunk_tri(tm):
    t = np.arange(tm)
    same = (t[:, None] // CHUNK) == (t[None, :] // CHUNK)
    lower = same & (t[None, :] <= t[:, None])
    return lower.astype(np.float32), same.astype(np.float32)


def _gdn_in(x, pre_gain, w_in, conv_w, a_log, dt_bias, *, tm=IN_TILE, col_chunk=256):
    B, L, D = x.shape
    W = N_HEADS * HEAD_DIM
    H = N_HEADS
    w_main = w_in[:, :4 * W].astype(BF16)
    wab = jnp.zeros((D, LANES), F32).at[:, :2 * H].set(w_in[:, 4 * W:4 * W + 2 * H]).astype(BF16)
    wabt = w_in[:, 4 * W:4 * W + 2 * H].T.astype(BF16)
    neg_a = -jnp.exp(a_log.astype(F32))
    hpc = jnp.zeros((2, LANES), F32).at[0, :H].set(neg_a).at[1, :H].set(dt_bias.astype(F32))
    hpr = jnp.zeros((2, 2 * H, tm), F32)
    hpr = hpr.at[0, :H, :].set(jnp.broadcast_to(neg_a[:, None], (H, tm)))
    hpr = hpr.at[1, :H, :].set(jnp.broadcast_to(dt_bias.astype(F32)[:, None], (H, tm)))
    lower, same = _chunk_tri(tm)
    lt = jnp.asarray(np.concatenate([lower, same], axis=0), BF16)
    ut = jnp.asarray(lower.T, BF16)

    kern = functools.partial(_gdn_in_kernel, tm=tm, width=W, col_chunk=col_chunk)
    row_spec = lambda n: pl.BlockSpec((1, tm, n), lambda b, i: (b, i, 0))
    head_spec = pl.BlockSpec((1, H, tm, HEAD_DIM), lambda b, i: (b, 0, i, 0))
    out_shape = (
        jax.ShapeDtypeStruct((B, H, L, HEAD_DIM), F32),
        jax.ShapeDtypeStruct((B, H, L, HEAD_DIM), F32),
        jax.ShapeDtypeStruct((B, H, L, HEAD_DIM), F32),
        jax.ShapeDtypeStruct((B, L, W), BF16),
        jax.ShapeDtypeStruct((B, H // SCAN_HP, L, LANES), F32),
        jax.ShapeDtypeStruct((B, H // SCAN_HP, 8, L), F32),
    )
    return pl.pallas_call(
        kern,
        grid=(B, L // tm),
        in_specs=[
            row_spec(D),
            _resident((1, D)),
            _resident((D, 4 * W)),
            _resident((D, LANES)),
            _resident((2 * H, D)),
            _resident((CONV_K, 3 * W)),
            _resident((2, LANES)),
            _resident((2, 2 * H, tm)),
            _resident((2 * tm, tm)),
            _resident((tm, tm)),
        ],
        out_specs=(head_spec, head_spec, head_spec, row_spec(W),
                   pl.BlockSpec((1, H // SCAN_HP, tm, LANES), lambda b, i: (b, 0, i, 0)),
                   pl.BlockSpec((1, H // SCAN_HP, 8, tm), lambda b, i: (b, 0, 0, i))),
        out_shape=out_shape,
        scratch_shapes=[
            pltpu.VMEM((tm, D), BF16),
            pltpu.VMEM((tm + 8, col_chunk), F32),
            pltpu.VMEM((8, 3 * W), F32),
        ],
        compiler_params=pltpu.CompilerParams(
            dimension_semantics=("arbitrary", "arbitrary"),
            vmem_limit_bytes=VMEM_LIMIT_BYTES),
        name="gdn_in",
    )(x, pre_gain.reshape(1, D).astype(F32), w_main, wab, wabt, conv_w.astype(F32), hpc, hpr, lt, ut)


def _gdn_scan_kernel(q_ref, k_ref, v_ref, gcol_ref, grow_ref, cmask_ref, smask_ref, eye_ref,
                     o_ref, s_scr, *, hp):
    P = PAIR
    n_pair = GROUP // P
    cpp = P // CHUNK

    @pl.when(pl.program_id(2) == 0)
    def _():
        s_scr[...] = jnp.zeros_like(s_scr)

    gcol = gcol_ref[0, 0]
    grow = grow_ref[0, 0]
    cmask = cmask_ref[...] > 0
    smask = smask_ref[...]
    eye = eye_ref[...]

    heads = range(hp)

    def setup(j):
        rows = slice(j * P, (j + 1) * P)
        st = dict(glb=[], kbg=[], vb=[], qg=[], kdec=[], attn=[], x=[], p=[])
        for h in heads:
            qf = _l2(_silu(q_ref[0, h, rows, :])) * (HEAD_DIM ** -0.5)
            kf = _l2(_silu(k_ref[0, h, rows, :]))
            vf = _silu(v_ref[0, h, rows, :])
            qh, kh = qf.astype(BF16), kf.astype(BF16)
            gc = jnp.broadcast_to(gcol[rows, h:h + 1], (P, LANES))
            beta = jnp.broadcast_to(gcol[rows, N_HEADS + h:N_HEADS + h + 1], (P, LANES))
            gl = jnp.broadcast_to(gcol[rows, 2 * N_HEADS + h:2 * N_HEADS + h + 1], (P, LANES))
            kb = kf * beta
            egc = jnp.exp(gc)
            st["glb"].append(gl)
            st["kbg"].append((kb * egc).astype(BF16))
            st["vb"].append((vf * beta).astype(BF16))
            st["qg"].append((qf * egc).astype(BF16))
            st["kdec"].append((kf * jnp.exp(gl - gc)).astype(BF16))
            kk = _dot_nt(kb.astype(BF16), kh)
            qk = _dot_nt(qh, kh)
            diff = gc - grow[h:h + 1, rows]
            dec = jnp.exp(jnp.where(cmask, diff, NEG_BIG))
            st["attn"].append((qk * dec).astype(BF16))
            st["x"].append(-(kk * dec * smask))
            st["p"].append(eye + st["x"][-1])
        return st

    def invert(st):
        x, p = st["x"], st["p"]
        xb = [x[h].astype(BF16) for h in heads]
        x = [_dot(xb[h], xb[h]) for h in heads]
        yield
        for _ in range(4):
            xb = [x[h].astype(BF16) for h in heads]
            r = [_dot(jnp.concatenate([xb[h], p[h].astype(BF16)], axis=0), xb[h]) for h in heads]
            x = [r[h][:P] for h in heads]
            p = [p[h] + r[h][P:] for h in heads]
            yield
        p = [p[h] + _dot(p[h].astype(BF16), x[h].astype(BF16)) for h in heads]
        yield
        uw = [_dot(p[h].astype(BF16), jnp.concatenate([st["vb"][h], st["kbg"][h]], axis=1)) for h in heads]
        st["u"] = [uw[h][:, :HEAD_DIM] for h in heads]
        st["wm"] = [uw[h][:, HEAD_DIM:].astype(BF16) for h in heads]
        yield

    def advance(st, j, s):
        vn_parts = [[] for _ in heads]
        for c in range(cpp):
            r0 = c * CHUNK
            wq = [_dot(jnp.concatenate([st["wm"][h][r0:r0 + CHUNK], st["qg"][h][r0:r0 + CHUNK]], axis=0),
                       s[h].astype(BF16)) for h in heads]
            yield
            vn = [(st["u"][h][r0:r0 + CHUNK] - wq[h][:CHUNK]).astype(BF16) for h in heads]
            pad = [jnp.zeros(((cpp - 1 - c) * CHUNK, HEAD_DIM), BF16)] if c < cpp - 1 else []
            for h in heads:
                vn_parts[h].append(vn[h])
                vn_all = jnp.concatenate(vn_parts[h] + pad, axis=0)
                o_c = wq[h][CHUNK:] + _dot(st["attn"][h][r0:r0 + CHUNK, :], vn_all)
                o_ref[0, h, j * P + r0:j * P + r0 + CHUNK, :] = o_c.astype(o_ref.dtype)
            yield
            for h in heads:
                s[h] = (s[h] * jnp.exp(st["glb"][h][r0:r0 + 1, :])
                        + _dot_tn(st["kdec"][h][r0:r0 + CHUNK], vn[h]))
            yield

    s = [s_scr[h] for h in heads]
    prev = None
    for j in range(n_pair):
        st = setup(j)
        gens = [invert(st)] + ([advance(prev, j - 1, s)] if prev is not None else [])
        while gens:
            for g in list(gens):
                if next(g, "done") == "done":
                    gens.remove(g)
        prev = st
    for _ in advance(prev, n_pair - 1, s):
        pass
    for h in heads:
        s_scr[h] = s[h]


def _pair_masks():
    t = np.arange(PAIR)
    same = (t[:, None] // CHUNK) == (t[None, :] // CHUNK)
    causal = same & (t[None, :] <= t[:, None])
    strict = same & (t[None, :] < t[:, None])
    return causal.astype(np.float32), strict.astype(np.float32), np.eye(PAIR, dtype=np.float32)


def _gdn_scan(q, k, v, gcol, grow):
    B, H, L, Dh = q.shape
    hp = SCAN_HP
    causal, strict, eye = _pair_masks()
    head_spec = pl.BlockSpec((1, hp, GROUP, Dh), lambda b, hg, g: (b, hg, g, 0))
    return pl.pallas_call(
        functools.partial(_gdn_scan_kernel, hp=hp),
        grid=(B, H // hp, L // GROUP),
        in_specs=[
            head_spec, head_spec, head_spec,
            pl.BlockSpec((1, 1, GROUP, LANES), lambda b, hg, g: (b, hg, g, 0)),
            pl.BlockSpec((1, 1, 8, GROUP), lambda b, hg, g: (b, hg, 0, g)),
            _resident((PAIR, PAIR)), _resident((PAIR, PAIR)), _resident((PAIR, PAIR)),
        ],
        out_specs=head_spec,
        out_shape=jax.ShapeDtypeStruct((B, H, L, Dh), BF16),
        scratch_shapes=[pltpu.VMEM((hp, Dh, Dh), F32)],
        compiler_params=pltpu.CompilerParams(
            dimension_semantics=("arbitrary", "arbitrary", "arbitrary"),
            vmem_limit_bytes=VMEM_LIMIT_BYTES),
        name="gdn_scan",
    )(q, k, v, gcol, grow, jnp.asarray(causal), jnp.asarray(strict), jnp.asarray(eye))


def _head_rms(xh, gain):
    ms = jnp.mean(xh * xh, axis=-1, keepdims=True)
    return xh * lax.rsqrt(ms + EPS) * gain


def _mid_kernel(o_ref, z_ref, x_ref, onorm_ref, wo_ref, post_ref,
                kvn_ref, wkt_ref, wv_ref, wf_ref, wft_ref, fb_ref, fbr_ref, kkn_ref,
                fpre_ref, wq_ref, wz_ref, qn_ref, lt_ref, ut_ref, selq_ref, qc_ref,
                h1_ref, kt_ref, v_ref, q_ref, z2_ref,
                carry_scr, carry_r_scr, *, tm):
    @pl.when(pl.program_id(1) == 0)
    def _():
        carry_scr[...] = jnp.zeros_like(carry_scr)
        carry_r_scr[...] = jnp.zeros_like(carry_r_scr)

    rb = MID_SUB
    n_split = tm // rb
    gated = []
    for r in range(n_split):
        rows = slice(r * rb, (r + 1) * rb)
        parts = []
        for h in range(N_HEADS):
            on = _head_rms(o_ref[0, h, rows, :].astype(F32), onorm_ref[...])
            zh = z_ref[0, rows, h * HEAD_DIM:(h + 1) * HEAD_DIM].astype(F32)
            parts.append((on * (zh * _sigmoid(zh))).astype(BF16))
        gated.append(jnp.concatenate(parts, axis=1))
    y = [_dot(gated[r], wo_ref[...]) for r in range(n_split)]
    u_parts, u2_parts = [], []
    for r in range(n_split):
        rows = slice(r * rb, (r + 1) * rb)
        h1_r = x_ref[0, rows, :] + _rms(y[r], post_ref[...])
        h1_ref[0, rows, :] = h1_r
        u_parts.append(_rms(h1_r, kvn_ref[...]).astype(BF16))
        u2_parts.append(_rms(h1_r, fpre_ref[...]).astype(BF16))

    u = jnp.concatenate(u_parts, axis=0)
    u2 = jnp.concatenate(u2_parts, axis=0)
    f = _dot(u, wf_ref[...]) + fb_ref[...]
    ft = _dot_nt(wft_ref[...], u) + fbr_ref[...]
    kt = _dot_nt(wkt_ref[...], u)
    vv = _dot(u, wv_ref[...])
    for h in range(N_HEADS):
        v_ref[0, h] = vv[:, h * HEAD_DIM:(h + 1) * HEAD_DIM].astype(BF16)

    lane = lax.broadcasted_iota(jnp.int32, f.shape, 1)
    logf = jnp.where(lane < N_HEADS, -_softplus(-f), 0.0)
    c = _dot3(lt_ref[...], logf) + carry_scr[0:1, :]
    carry_scr[...] = jnp.broadcast_to(c[tm - 1:tm, :], carry_scr.shape)
    c_hi, c_mid, c_lo = _split3(c * LOG2E)
    packed = (c_hi.astype(F32) + pltpu.roll(c_mid.astype(F32), N_HEADS, axis=1)
              + pltpu.roll(c_lo.astype(F32), 2 * N_HEADS, axis=1)).astype(BF16)
    qaug = _dot(packed, selq_ref[...]) + qc_ref[...]

    row = lax.broadcasted_iota(jnp.int32, ft.shape, 0)
    logf_r = jnp.where(row < N_HEADS, -_softplus(-ft), 0.0)
    c_r = _dot3_r(logf_r, ut_ref[...]) + carry_r_scr[...]
    carry_r_scr[...] = jnp.broadcast_to(c_r[:, tm - 1:tm], carry_r_scr.shape)
    r_hi, r_mid, r_lo = _split3(c_r * LOG2E)
    r_hi, r_mid, r_lo = r_hi.astype(F32), r_mid.astype(F32), r_lo.astype(F32)
    qq = _dot(u2, wq_ref[...])
    zz = _dot(u2, wz_ref[...])

    sub = lax.broadcasted_iota(jnp.int32, (16, tm), 0)
    for h in range(N_HEADS):
        kh = kt[h * HEAD_DIM:(h + 1) * HEAD_DIM, :]
        ms = jnp.mean(kh * kh, axis=0, keepdims=True)
        khn = (kh * lax.rsqrt(ms + EPS) * kkn_ref[...]).astype(BF16)
        aug = jnp.where(sub < 3, 1.0,
                        jnp.where(sub == 3, -r_hi[h:h + 1],
                                  jnp.where(sub == 4, -r_mid[h:h + 1],
                                            jnp.where(sub == 5, -r_lo[h:h + 1], 0.0)))).astype(BF16)
        for sl in range(tm // SLAB):
            cols = slice(sl * SLAB, (sl + 1) * SLAB)
            kt_ref[0, h, sl, 0:HEAD_DIM, :] = khn[:, cols]
            kt_ref[0, h, sl, HEAD_DIM:HEAD_DIM + 16, :] = aug[:, cols]
            kt_ref[0, h, sl, HEAD_DIM + 16:2 * HEAD_DIM, :] = jnp.zeros((HEAD_DIM - 16, SLAB), BF16)

    q_scale = HEAD_DIM ** -0.5 * LOG2E
    for h in range(N_HEADS):
        qh = _head_rms(qq[:, h * HEAD_DIM:(h + 1) * HEAD_DIM], qn_ref[...])
        q_ref[0, h, :, 0:HEAD_DIM] = (qh * q_scale).astype(BF16)
        q_ref[0, h, :, HEAD_DIM:2 * HEAD_DIM] = qaug[:, h * HEAD_DIM:(h + 1) * HEAD_DIM].astype(BF16)
    z2_ref[0] = zz.astype(BF16)


def _aug_selectors():
    selq = np.zeros((LANES, N_HEADS * HEAD_DIM), np.float32)
    qc = np.zeros((1, N_HEADS * HEAD_DIM), np.float32)
    for h in range(N_HEADS):
        for t in range(3):
            selq[t * N_HEADS + h, h * HEAD_DIM + t] = 1.0
            qc[0, h * HEAD_DIM + 3 + t] = 1.0
    return selq, qc


def _mid(o, z, x, o_norm, w_out, post_gain, kv_norm, kv_w, kv_fbias, kv_k_norm,
         fox_pre, fox_w_in, fox_q_norm, *, tm=ROW_TILE):
    B, L, D = x.shape
    H, W = N_HEADS, N_HEADS * HEAD_DIM
    assert tm % SLAB == 0
    wo = w_out.astype(BF16)
    wkt = kv_w[:, :W].T.astype(BF16)
    wv = kv_w[:, W:2 * W].astype(BF16)
    wf = jnp.zeros((D, LANES), F32).at[:, :H].set(kv_w[:, 2 * W:2 * W + H]).astype(BF16)
    wft = jnp.zeros((2 * H, D), F32).at[:H, :].set(kv_w[:, 2 * W:2 * W + H].T).astype(BF16)
    fb = jnp.zeros((1, LANES), F32).at[0, :H].set(kv_fbias.astype(F32))
    fbr = jnp.zeros((2 * H, tm), F32).at[:H, :].set(
        jnp.broadcast_to(kv_fbias.astype(F32)[:, None], (H, tm)))
    kkn = jnp.broadcast_to(kv_k_norm.astype(F32)[:, None], (HEAD_DIM, tm))
    wq = fox_w_in[:, :W].astype(BF16)
    wz = fox_w_in[:, W:2 * W].astype(BF16)
    tri = np.tril(np.ones((tm, tm), np.float32))
    selq, qc = _aug_selectors()
    row = lambda n: pl.BlockSpec((1, tm, n), lambda b, i: (b, i, 0))
    head_spec = lambda n: pl.BlockSpec((1, H, tm, n), lambda b, i: (b, 0, i, 0))
    vec = lambda a, n: a.reshape(1, n).astype(F32)
    out_shape = (
        jax.ShapeDtypeStruct((B, L, D), F32),
        jax.ShapeDtypeStruct((B, H, L // SLAB, 2 * HEAD_DIM, SLAB), BF16),
        jax.ShapeDtypeStruct((B, H, L, HEAD_DIM), BF16),
        jax.ShapeDtypeStruct((B, H, L, 2 * HEAD_DIM), BF16),
        jax.ShapeDtypeStruct((B, L, W), BF16),
    )
    return pl.pallas_call(
        functools.partial(_mid_kernel, tm=tm),
        grid=(B, L // tm),
        in_specs=[
            head_spec(HEAD_DIM), row(W), row(D),
            _resident((1, HEAD_DIM)), _resident((W, D)), _resident((1, D)),
            _resident((1, D)), _resident((W, D)), _resident((D, W)), _resident((D, LANES)),
            _resident((2 * H, D)), _resident((1, LANES)), _resident((2 * H, tm)),
            _resident((HEAD_DIM, tm)),
            _resident((1, D)), _resident((D, W)), _resident((D, W)), _resident((1, HEAD_DIM)),
            _resident((tm, tm)), _resident((tm, tm)), _resident((LANES, W)), _resident((1, W)),
        ],
        out_specs=(row(D),
                   pl.BlockSpec((1, H, tm // SLAB, 2 * HEAD_DIM, SLAB), lambda b, i: (b, 0, i, 0, 0)),
                   head_spec(HEAD_DIM), head_spec(2 * HEAD_DIM), row(W)),
        out_shape=out_shape,
        scratch_shapes=[pltpu.VMEM((8, LANES), F32), pltpu.VMEM((2 * H, tm), F32)],
        compiler_params=pltpu.CompilerParams(
            dimension_semantics=("arbitrary", "arbitrary"),
            vmem_limit_bytes=VMEM_LIMIT_BYTES),
        name="mid",
    )(o, z, x, vec(o_norm, HEAD_DIM), wo, vec(post_gain, D),
      vec(kv_norm, D), wkt, wv, wf, wft, fb, fbr, kkn,
      vec(fox_pre, D), wq, wz, vec(fox_q_norm, HEAD_DIM),
      jnp.asarray(tri, BF16), jnp.asarray(tri.T, BF16), jnp.asarray(selq, BF16), jnp.asarray(qc))


def _fox_attn_kernel(q_ref, kt_ref, v_ref, o_ref, acc_scr, m_scr, *, tq, hp):
    ts = SLAB
    i = pl.program_id(2)
    n_diag = tq // ts
    lane = lax.broadcasted_iota(jnp.int32, (ts, HEAD_DIM), 1)
    ones_col = jnp.where(lane == 0, 1.0, 0.0).astype(BF16)
    col = lax.broadcasted_iota(jnp.int32, (2 * ts, ts), 1)
    rowi = lax.broadcasted_iota(jnp.int32, (2 * ts, ts), 0)
    pair_masks = (col <= rowi, col + ts <= rowi)

    def v_aug(h, j):
        r0 = pl.multiple_of(j * ts, ts)
        return jnp.concatenate([v_ref[0, h, pl.ds(r0, ts), :], ones_col], axis=1)

    def scores(h, slabs, row0):
        q = q_ref[0, h, row0:tq, :]
        return [_dot(q, kt_ref[0, h, j]) for j in slabs]

    def finish(h, parts, slabs, row0, diag):
        if diag:
            parts = [jnp.concatenate([jnp.where(mk, p[:2 * ts], NEG_BIG), p[2 * ts:]], axis=0)
                     if p.shape[0] > 2 * ts else jnp.where(mk, p, NEG_BIG)
                     for p, mk in zip(parts, pair_masks)]
        mx = parts[0]
        for p in parts[1:]:
            mx = jnp.maximum(mx, p)
        m_old = m_scr[h, row0:tq, :]
        m_new = jnp.maximum(m_old, jnp.max(mx, axis=1, keepdims=True))
        m_scr[h, row0:tq, :] = m_new
        alpha = jnp.exp2(m_old - m_new)
        m_wide = jnp.concatenate([m_new, m_new], axis=1)
        pv = None
        for p, j in zip(parts, slabs):
            d = _dot(jnp.exp2(p - m_wide).astype(BF16), v_aug(h, j))
            pv = d if pv is None else pv + d
        acc_scr[h, row0:tq, :] = jnp.concatenate([alpha, alpha], axis=1) * acc_scr[h, row0:tq, :] + pv

    acc_scr[...] = jnp.zeros_like(acc_scr)
    m_scr[...] = jnp.full(m_scr.shape, NEG_BIG, F32)

    def step(slabs, row0, diag):
        parts = [scores(h, slabs, row0) for h in range(hp)]
        for h in range(hp):
            finish(h, parts[h], slabs, row0, diag)

    def full_blocks(jj, carry):
        step((2 * jj, 2 * jj + 1), 0, False)
        return carry

    lax.fori_loop(0, i * (n_diag // 2), full_blocks, 0)
    for d in range(n_diag // 2):
        step((i * n_diag + 2 * d, i * n_diag + 2 * d + 1), 2 * d * ts, True)
    for h in range(hp):
        acc = acc_scr[h]
        o_ref[0, h] = (acc[:, :HEAD_DIM] / acc[:, HEAD_DIM:HEAD_DIM + 1]).astype(o_ref.dtype)


def _fox_attn(q2, kt2, v, *, tq=1024, hp=4):
    B, H, L, Dh = v.shape
    tq = min(tq, L)
    assert tq % (2 * SLAB) == 0 and L % tq == 0 and H % hp == 0
    return pl.pallas_call(
        functools.partial(_fox_attn_kernel, tq=tq, hp=hp),
        grid=(B, H // hp, L // tq),
        in_specs=[pl.BlockSpec((1, hp, tq, 2 * Dh), lambda b, h, i: (b, h, i, 0)),
                  pl.BlockSpec((1, hp, L // SLAB, 2 * Dh, SLAB), lambda b, h, i: (b, h, 0, 0, 0)),
                  pl.BlockSpec((1, hp, L, Dh), lambda b, h, i: (b, h, 0, 0))],
        out_specs=pl.BlockSpec((1, hp, tq, Dh), lambda b, h, i: (b, h, i, 0)),
        out_shape=jax.ShapeDtypeStruct((B, H, L, Dh), BF16),
        scratch_shapes=[pltpu.VMEM((hp, tq, 2 * Dh), F32), pltpu.VMEM((hp, tq, LANES), F32)],
        compiler_params=pltpu.CompilerParams(
            dimension_semantics=("arbitrary", "arbitrary", "arbitrary"),
            vmem_limit_bytes=VMEM_LIMIT_BYTES),
        name="fox_attn",
    )(q2, kt2, v)


def _fox_out_kernel(o_ref, z_ref, h_ref, wo_ref, post_ref, out_ref):
    gated = []
    for h in range(N_HEADS):
        zh = z_ref[0, :, h * HEAD_DIM:(h + 1) * HEAD_DIM].astype(F32)
        gated.append((o_ref[0, h].astype(F32) * (zh * _sigmoid(zh))).astype(BF16))
    y = _dot(jnp.concatenate(gated, axis=1), wo_ref[...])
    out_ref[0] = h_ref[0] + _rms(y, post_ref[...])


def _fox_out(o, z2, h1, w_out, post_gain, *, tm=ROW_TILE):
    B, L, D = h1.shape
    H, W = N_HEADS, N_HEADS * HEAD_DIM
    row = lambda n: pl.BlockSpec((1, tm, n), lambda b, i: (b, i, 0))
    return pl.pallas_call(
        _fox_out_kernel,
        grid=(B, L // tm),
        in_specs=[pl.BlockSpec((1, H, tm, HEAD_DIM), lambda b, i: (b, 0, i, 0)), row(W), row(D),
                  _resident((W, D)), _resident((1, D))],
        out_specs=row(D),
        out_shape=jax.ShapeDtypeStruct((B, L, D), F32),
        compiler_params=pltpu.CompilerParams(
            dimension_semantics=("arbitrary", "arbitrary"),
            vmem_limit_bytes=VMEM_LIMIT_BYTES),
        name="fox_out",
    )(o, z2, h1, w_out.astype(BF16), post_gain.reshape(1, D).astype(F32))


def kernel(x, gdn_pre_norm, gdn_w_in, gdn_conv_w, gdn_a_log, gdn_dt_bias, gdn_o_norm, gdn_w_out, gdn_post_norm, kv_norm, kv_w, kv_forget_bias, kv_k_norm, fox_pre_norm, fox_w_in, fox_q_norm, fox_w_out, fox_post_norm):
    assert gdn_w_in.shape[0] == 1 and fox_w_in.shape[0] == 1, "one GDN layer then one FoX layer"
    q, k, v, z, gcol, grow = _gdn_in(x, gdn_pre_norm[0], gdn_w_in[0], gdn_conv_w[0],
                                     gdn_a_log[0], gdn_dt_bias[0])
    o = _gdn_scan(q, k, v, gcol, grow)
    h1, kt2, vs, q2, z2 = _mid(
        o, z, x, gdn_o_norm[0], gdn_w_out[0], gdn_post_norm[0],
        kv_norm, kv_w, kv_forget_bias, kv_k_norm, fox_pre_norm[0], fox_w_in[0], fox_q_norm[0])
    o2 = _fox_attn(q2, kt2, vs)
    return _fox_out(o2, z2, h1, fox_w_out[0], fox_post_norm[0])
```

```python
import functools
import math

import jax
import jax.numpy as jnp
import numpy as np
from jax import lax
from jax.experimental import pallas as pl
from jax.experimental.pallas import tpu as pltpu

HEAD_DIM = 128
N_HEADS = 8
CONV_K = 4
CHUNK = 64
GROUP = 1024
IN_TILE = 256
PAIR = 128
SCAN_HP = 8
SLAB = 256
ROW_TILE = 512
MID_SUB = 128
LOG2E = math.log2(math.e)
EPS = 1e-6
LANES = 128
VMEM_LIMIT_BYTES = 56 * 1024 * 1024
NEG_BIG = -1e30

F32 = jnp.float32
BF16 = jnp.bfloat16


def _dot(a, b):
    return jnp.dot(a, b, preferred_element_type=F32)


def _dot_nt(a, b):
    return lax.dot_general(a, b, (((1,), (1,)), ((), ())), preferred_element_type=F32)


def _dot_tn(a, b):
    return lax.dot_general(a, b, (((0,), (0,)), ((), ())), preferred_element_type=F32)


def _split3(x):
    hi = x.astype(BF16)
    r1 = x - hi.astype(F32)
    mid = r1.astype(BF16)
    lo = (r1 - mid.astype(F32)).astype(BF16)
    return hi, mid, lo


def _dot3(a_bf16, x):
    hi, mid, lo = _split3(x)
    return _dot(a_bf16, hi) + _dot(a_bf16, mid) + _dot(a_bf16, lo)


def _dot3_r(x, b_bf16):
    hi, mid, lo = _split3(x)
    return _dot(hi, b_bf16) + _dot(mid, b_bf16) + _dot(lo, b_bf16)


def _sigmoid(x):
    return 1.0 / (1.0 + jnp.exp(-x))


def _softplus(x):
    return jnp.maximum(x, 0.0) + jnp.log(1.0 + jnp.exp(-jnp.abs(x)))


def _silu(x):
    return x * _sigmoid(x)


def _l2(x):
    return x * lax.rsqrt(jnp.sum(x * x, axis=-1, keepdims=True) + EPS)


def _rms(x, gain):
    ms = jnp.mean(x * x, axis=-1, keepdims=True)
    return x * lax.rsqrt(ms + EPS) * gain


def _resident(shape, col_block=0):
    nd = len(shape)
    index = (0,) * (nd - 1) + (col_block,)
    return pl.BlockSpec(shape, lambda *_: index, pipeline_mode=pl.Buffered(1))


def _gdn_in_kernel(x_ref, gain_ref, w_ref, wab_ref, wabt_ref, convw_ref, hpc_ref, hpr_ref,
                   lt_ref, ut_ref,
                   q_ref, k_ref, v_ref, z_ref, gcol_ref, grow_ref,
                   xn_scr, pre_scr, carry_scr, *, tm, width, col_chunk):
    @pl.when(pl.program_id(1) == 0)
    def _():
        carry_scr[...] = jnp.zeros_like(carry_scr)

    xn_scr[...] = _rms(x_ref[0], gain_ref[...]).astype(BF16)
    xn = xn_scr[...]

    def z_chunk(c):
        lo = 3 * width + c * col_chunk
        z_ref[0, :, c * col_chunk:(c + 1) * col_chunk] = _dot(xn, w_ref[:, lo:lo + col_chunk]).astype(BF16)

    n_qkv = 3 * width // col_chunk
    n_z = width // col_chunk
    z_after = {(i + 1) * n_qkv // (n_z + 1) - 1: i for i in range(n_z)}

    def conv_chunk(c):
        lo = c * col_chunk
        pc = _dot(xn, w_ref[:, lo:lo + col_chunk])
        pre_scr[0:8, :] = carry_scr[:, lo:lo + col_chunk]
        pre_scr[8:8 + tm, :] = pc
        carry_scr[:, lo:lo + col_chunk] = pc[tm - 8:tm, :]
        cw = convw_ref[:, lo:lo + col_chunk]
        acc = pc * cw[CONV_K - 1:CONV_K, :]
        for j in range(1, CONV_K):
            acc = acc + pre_scr[8 - j:8 - j + tm, :] * cw[CONV_K - 1 - j:CONV_K - j, :]
        for hh in range(col_chunk // HEAD_DIM):
            col = lo + hh * HEAD_DIM
            out_ref = (q_ref, k_ref, v_ref)[col // width]
            out_ref[0, (col % width) // HEAD_DIM] = acc[:, hh * HEAD_DIM:(hh + 1) * HEAD_DIM]
        if c in z_after:
            z_chunk(z_after[c])

    for c in range(n_qkv // 2):
        conv_chunk(c)

    ab = _dot(xn, wab_ref[...])
    lane = lax.broadcasted_iota(jnp.int32, ab.shape, 1)
    g_col = jnp.where(lane < N_HEADS, hpc_ref[0:1, :] * _softplus(ab + hpc_ref[1:2, :]), 0.0)
    beta_col = _sigmoid(ab)
    cums = _dot3(lt_ref[...], g_col)
    gc_col = cums[0:tm]
    gl_col = pltpu.roll(cums[tm:2 * tm], 2 * N_HEADS, axis=1)
    gates = jnp.where(lane < N_HEADS, gc_col, jnp.where(lane < 2 * N_HEADS, beta_col, gl_col))
    for hg in range(N_HEADS // SCAN_HP):
        shift = (LANES - hg * SCAN_HP) % LANES
        gcol_ref[0, hg] = gates if shift == 0 else pltpu.roll(gates, shift, axis=1)

    abt = _dot_nt(wabt_ref[...], xn)
    row = lax.broadcasted_iota(jnp.int32, abt.shape, 0)
    g_row = jnp.where(row < N_HEADS, hpr_ref[0] * _softplus(abt + hpr_ref[1]), 0.0)
    gc_row = _dot3_r(g_row, ut_ref[...])
    for hg in range(N_HEADS // SCAN_HP):
        grow_ref[0, hg] = gc_row[hg * SCAN_HP:hg * SCAN_HP + 8, :]

    for c in range(n_qkv // 2, n_qkv):
        conv_chunk(c)


def _chunk_tri(tm):
    t = np.arange(tm)
    same = (t[:, None] // CHUNK) == (t[None, :] // CHUNK)
    lower = same & (t[None, :] <= t[:, None])
    return lower.astype(np.float32), same.astype(np.float32)


def _gdn_in(x, pre_gain, w_in, conv_w, a_log, dt_bias, *, tm=IN_TILE, col_chunk=256):
    B, L, D = x.shape
    W = N_HEADS * HEAD_DIM
    H = N_HEADS
    w_main = w_in.astype(BF16)
    wab = jnp.zeros((D, LANES), F32).at[:, :2 * H].set(w_in[:, 4 * W:4 * W + 2 * H]).astype(BF16)
    wabt = w_in[:, 4 * W:4 * W + 2 * H].T.astype(BF16)
    neg_a = -jnp.exp(a_log.astype(F32))
    hpc = jnp.zeros((2, LANES), F32).at[0, :H].set(neg_a).at[1, :H].set(dt_bias.astype(F32))
    hpr = jnp.zeros((2, 2 * H, tm), F32)
    hpr = hpr.at[0, :H, :].set(jnp.broadcast_to(neg_a[:, None], (H, tm)))
    hpr = hpr.at[1, :H, :].set(jnp.broadcast_to(dt_bias.astype(F32)[:, None], (H, tm)))
    lower, same = _chunk_tri(tm)
    lt = jnp.asarray(np.concatenate([lower, same], axis=0), BF16)
    ut = jnp.asarray(lower.T, BF16)

    kern = functools.partial(_gdn_in_kernel, tm=tm, width=W, col_chunk=col_chunk)
    row_spec = lambda n: pl.BlockSpec((1, tm, n), lambda b, i: (b, i, 0))
    head_spec = pl.BlockSpec((1, H, tm, HEAD_DIM), lambda b, i: (b, 0, i, 0))
    out_shape = (
        jax.ShapeDtypeStruct((B, H, L, HEAD_DIM), F32),
        jax.ShapeDtypeStruct((B, H, L, HEAD_DIM), F32),
        jax.ShapeDtypeStruct((B, H, L, HEAD_DIM), F32),
        jax.ShapeDtypeStruct((B, L, W), BF16),
        jax.ShapeDtypeStruct((B, H // SCAN_HP, L, LANES), F32),
        jax.ShapeDtypeStruct((B, H // SCAN_HP, 8, L), F32),
    )
    return pl.pallas_call(
        kern,
        grid=(B, L // tm),
        in_specs=[
            row_spec(D),
            _resident((1, D)),
            _resident((D, 4 * W)),
            _resident((D, LANES)),
            _resident((2 * H, D)),
            _resident((CONV_K, 3 * W)),
            _resident((2, LANES)),
            _resident((2, 2 * H, tm)),
            _resident((2 * tm, tm)),
            _resident((tm, tm)),
        ],
        out_specs=(head_spec, head_spec, head_spec, row_spec(W),
                   pl.BlockSpec((1, H // SCAN_HP, tm, LANES), lambda b, i: (b, 0, i, 0)),
                   pl.BlockSpec((1, H // SCAN_HP, 8, tm), lambda b, i: (b, 0, 0, i))),
        out_shape=out_shape,
        scratch_shapes=[
            pltpu.VMEM((tm, D), BF16),
            pltpu.VMEM((tm + 8, col_chunk), F32),
            pltpu.VMEM((8, 3 * W), F32),
        ],
        compiler_params=pltpu.CompilerParams(
            dimension_semantics=("arbitrary", "arbitrary"),
            vmem_limit_bytes=VMEM_LIMIT_BYTES),
        name="gdn_in",
    )(x, pre_gain.reshape(1, D).astype(F32), w_main, wab, wabt, conv_w.astype(F32), hpc, hpr, lt, ut)


def _gdn_scan_kernel(q_ref, k_ref, v_ref, gcol_ref, grow_ref, cmask_ref, smask_ref, eye_ref,
                     o_ref, s_scr, *, hp):
    P = PAIR
    n_pair = GROUP // P
    cpp = P // CHUNK

    @pl.when(pl.program_id(2) == 0)
    def _():
        s_scr[...] = jnp.zeros_like(s_scr)

    gcol = gcol_ref[0, 0]
    grow = grow_ref[0, 0]
    cmask = cmask_ref[...] > 0
    smask = smask_ref[...]
    eye = eye_ref[...]

    heads = range(hp)

    def setup(j):
        rows = slice(j * P, (j + 1) * P)
        st = dict(glb=[], kbg=[], vb=[], qg=[], kdec=[], attn=[], x=[], p=[])
        for h in heads:
            qf = _l2(_silu(q_ref[0, h, rows, :])) * (HEAD_DIM ** -0.5)
            kf = _l2(_silu(k_ref[0, h, rows, :]))
            vf = _silu(v_ref[0, h, rows, :])
            qh, kh = qf.astype(BF16), kf.astype(BF16)
            gc = jnp.broadcast_to(gcol[rows, h:h + 1], (P, LANES))
            beta = jnp.broadcast_to(gcol[rows, N_HEADS + h:N_HEADS + h + 1], (P, LANES))
            gl = jnp.broadcast_to(gcol[rows, 2 * N_HEADS + h:2 * N_HEADS + h + 1], (P, LANES))
            kb = kf * beta
            egc = jnp.exp(gc)
            st["glb"].append(gl)
            st["kbg"].append((kb * egc).astype(BF16))
            st["vb"].append((vf * beta).astype(BF16))
            st["qg"].append((qf * egc).astype(BF16))
            st["kdec"].append((kf * jnp.exp(gl - gc)).astype(BF16))
            kk = _dot_nt(kb.astype(BF16), kh)
            qk = _dot_nt(qh, kh)
            diff = gc - grow[h:h + 1, rows]
            dec = jnp.exp(jnp.where(cmask, diff, NEG_BIG))
            st["attn"].append((qk * dec).astype(BF16))
            st["x"].append(-(kk * dec * smask))
            st["p"].append(eye + st["x"][-1])
        return st

    def invert(st):
        x, p = st["x"], st["p"]
        xb = [x[h].astype(BF16) for h in heads]
        x = [_dot(xb[h], xb[h]) for h in heads]
        yield
        for _ in range(4):
            xb = [x[h].astype(BF16) for h in heads]
            r = [_dot(jnp.concatenate([xb[h], p[h].astype(BF16)], axis=0), xb[h]) for h in heads]
            x = [r[h][:P] for h in heads]
            p = [p[h] + r[h][P:] for h in heads]
            yield
        p = [p[h] + _dot(p[h].astype(BF16), x[h].astype(BF16)) for h in heads]
        yield
        uw = [_dot(p[h].astype(BF16), jnp.concatenate([st["vb"][h], st["kbg"][h]], axis=1)) for h in heads]
        st["u"] = [uw[h][:, :HEAD_DIM] for h in heads]
        st["wm"] = [uw[h][:, HEAD_DIM:].astype(BF16) for h in heads]
        yield

    def advance(st, j, s):
        vn_parts = [[] for _ in heads]
        for c in range(cpp):
            r0 = c * CHUNK
            wq = [_dot(jnp.concatenate([st["wm"][h][r0:r0 + CHUNK], st["qg"][h][r0:r0 + CHUNK]], axis=0),
                       s[h].astype(BF16)) for h in heads]
            yield
            vn = [(st["u"][h][r0:r0 + CHUNK] - wq[h][:CHUNK]).astype(BF16) for h in heads]
            pad = [jnp.zeros(((cpp - 1 - c) * CHUNK, HEAD_DIM), BF16)] if c < cpp - 1 else []
            for h in heads:
                vn_parts[h].append(vn[h])
                vn_all = jnp.concatenate(vn_parts[h] + pad, axis=0)
                o_c = wq[h][CHUNK:] + _dot(st["attn"][h][r0:r0 + CHUNK, :], vn_all)
                o_ref[0, h, j * P + r0:j * P + r0 + CHUNK, :] = o_c.astype(o_ref.dtype)
            yield
            for h in heads:
                s[h] = (s[h] * jnp.exp(st["glb"][h][r0:r0 + 1, :])
                        + _dot_tn(st["kdec"][h][r0:r0 + CHUNK], vn[h]))
            yield

    s = [s_scr[h] for h in heads]
    prev = None
    for j in range(n_pair):
        st = setup(j)
        gens = [invert(st)] + ([advance(prev, j - 1, s)] if prev is not None else [])
        while gens:
            for g in list(gens):
                if next(g, "done") == "done":
                    gens.remove(g)
        prev = st
    for _ in advance(prev, n_pair - 1, s):
        pass
    for h in heads:
        s_scr[h] = s[h]


def _pair_masks():
    t = np.arange(PAIR)
    same = (t[:, None] // CHUNK) == (t[None, :] // CHUNK)
    causal = same & (t[None, :] <= t[:, None])
    strict = same & (t[None, :] < t[:, None])
    return causal.astype(np.float32), strict.astype(np.float32), np.eye(PAIR, dtype=np.float32)


def _gdn_scan(q, k, v, gcol, grow):
    B, H, L, Dh = q.shape
    hp = SCAN_HP
    causal, strict, eye = _pair_masks()
    head_spec = pl.BlockSpec((1, hp, GROUP, Dh), lambda b, hg, g: (b, hg, g, 0))
    return pl.pallas_call(
        functools.partial(_gdn_scan_kernel, hp=hp),
        grid=(B, H // hp, L // GROUP),
        in_specs=[
            head_spec, head_spec, head_spec,
            pl.BlockSpec((1, 1, GROUP, LANES), lambda b, hg, g: (b, hg, g, 0)),
            pl.BlockSpec((1, 1, 8, GROUP), lambda b, hg, g: (b, hg, 0, g)),
            _resident((PAIR, PAIR)), _resident((PAIR, PAIR)), _resident((PAIR, PAIR)),
        ],
        out_specs=head_spec,
        out_shape=jax.ShapeDtypeStruct((B, H, L, Dh), BF16),
        scratch_shapes=[pltpu.VMEM((hp, Dh, Dh), F32)],
        compiler_params=pltpu.CompilerParams(
            dimension_semantics=("arbitrary", "arbitrary", "arbitrary"),
            vmem_limit_bytes=VMEM_LIMIT_BYTES),
        name="gdn_scan",
    )(q, k, v, gcol, grow, jnp.asarray(causal), jnp.asarray(strict), jnp.asarray(eye))


def _head_rms(xh, gain):
    ms = jnp.mean(xh * xh, axis=-1, keepdims=True)
    return xh * lax.rsqrt(ms + EPS) * gain


def _mid_kernel(o_ref, z_ref, x_ref, onorm_ref, wo_ref, post_ref,
                kvn_ref, wkt_ref, wv_ref, wf_ref, wft_ref, fb_ref, fbr_ref, kkn_ref,
                fpre_ref, wq_ref, wz_ref, qn_ref, lt_ref, ut_ref, selq_ref, qc_ref,
                h1_ref, kt_ref, v_ref, q_ref, z2_ref,
                carry_scr, carry_r_scr, *, tm):
    @pl.when(pl.program_id(1) == 0)
    def _():
        carry_scr[...] = jnp.zeros_like(carry_scr)
        carry_r_scr[...] = jnp.zeros_like(carry_r_scr)

    rb = MID_SUB
    n_split = tm // rb
    gated = []
    for r in range(n_split):
        rows = slice(r * rb, (r + 1) * rb)
        parts = []
        for h in range(N_HEADS):
            on = _head_rms(o_ref[0, h, rows, :].astype(F32), onorm_ref[...])
            zh = z_ref[0, rows, h * HEAD_DIM:(h + 1) * HEAD_DIM].astype(F32)
            parts.append((on * (zh * _sigmoid(zh))).astype(BF16))
        gated.append(jnp.concatenate(parts, axis=1))
    y = [_dot(gated[r], wo_ref[...]) for r in range(n_split)]
    u_parts, u2_parts = [], []
    for r in range(n_split):
        rows = slice(r * rb, (r + 1) * rb)
        h1_r = x_ref[0, rows, :] + _rms(y[r], post_ref[...])
        h1_ref[0, rows, :] = h1_r
        u_parts.append(_rms(h1_r, kvn_ref[...]).astype(BF16))
        u2_parts.append(_rms(h1_r, fpre_ref[...]).astype(BF16))

    u = jnp.concatenate(u_parts, axis=0)
    u2 = jnp.concatenate(u2_parts, axis=0)
    f = _dot(u, wf_ref[...]) + fb_ref[...]
    ft = _dot_nt(wft_ref[...], u) + fbr_ref[...]
    kt = _dot_nt(wkt_ref[...], u)
    vv = _dot(u, wv_ref[...])
    for h in range(N_HEADS):
        v_ref[0, h] = vv[:, h * HEAD_DIM:(h + 1) * HEAD_DIM].astype(BF16)

    lane = lax.broadcasted_iota(jnp.int32, f.shape, 1)
    logf = jnp.where(lane < N_HEADS, -_softplus(-f), 0.0)
    c = _dot3(lt_ref[...], logf) + carry_scr[0:1, :]
    carry_scr[...] = jnp.broadcast_to(c[tm - 1:tm, :], carry_scr.shape)
    c_hi, c_mid, c_lo = _split3(c * LOG2E)
    packed = (c_hi.astype(F32) + pltpu.roll(c_mid.astype(F32), N_HEADS, axis=1)
              + pltpu.roll(c_lo.astype(F32), 2 * N_HEADS, axis=1)).astype(BF16)
    qaug = _dot(packed, selq_ref[...]) + qc_ref[...]

    row = lax.broadcasted_iota(jnp.int32, ft.shape, 0)
    logf_r = jnp.where(row < N_HEADS, -_softplus(-ft), 0.0)
    c_r = _dot3_r(logf_r, ut_ref[...]) + carry_r_scr[...]
    carry_r_scr[...] = jnp.broadcast_to(c_r[:, tm - 1:tm], carry_r_scr.shape)
    r_hi, r_mid, r_lo = _split3(c_r * LOG2E)
    r_hi, r_mid, r_lo = r_hi.astype(F32), r_mid.astype(F32), r_lo.astype(F32)
    qq = _dot(u2, wq_ref[...])
    zz = _dot(u2, wz_ref[...])

    sub = lax.broadcasted_iota(jnp.int32, (16, tm), 0)
    for h in range(N_HEADS):
        kh = kt[h * HEAD_DIM:(h + 1) * HEAD_DIM, :]
        ms = jnp.mean(kh * kh, axis=0, keepdims=True)
        khn = (kh * lax.rsqrt(ms + EPS) * kkn_ref[...]).astype(BF16)
        aug = jnp.where(sub < 3, 1.0,
                        jnp.where(sub == 3, -r_hi[h:h + 1],
                                  jnp.where(sub == 4, -r_mid[h:h + 1],
                                            jnp.where(sub == 5, -r_lo[h:h + 1], 0.0)))).astype(BF16)
        for sl in range(tm // SLAB):
            cols = slice(sl * SLAB, (sl + 1) * SLAB)
            kt_ref[0, h, sl, 0:HEAD_DIM, :] = khn[:, cols]
            kt_ref[0, h, sl, HEAD_DIM:HEAD_DIM + 16, :] = aug[:, cols]
            kt_ref[0, h, sl, HEAD_DIM + 16:2 * HEAD_DIM, :] = jnp.zeros((HEAD_DIM - 16, SLAB), BF16)

    q_scale = HEAD_DIM ** -0.5 * LOG2E
    for h in range(N_HEADS):
        qh = _head_rms(qq[:, h * HEAD_DIM:(h + 1) * HEAD_DIM], qn_ref[...])
        q_ref[0, h, :, 0:HEAD_DIM] = (qh * q_scale).astype(BF16)
        q_ref[0, h, :, HEAD_DIM:2 * HEAD_DIM] = qaug[:, h * HEAD_DIM:(h + 1) * HEAD_DIM].astype(BF16)
    z2_ref[0] = zz.astype(BF16)


def _aug_selectors():
    selq = np.zeros((LANES, N_HEADS * HEAD_DIM), np.float32)
    qc = np.zeros((1, N_HEADS * HEAD_DIM), np.float32)
    for h in range(N_HEADS):
        for t in range(3):
            selq[t * N_HEADS + h, h * HEAD_DIM + t] = 1.0
            qc[0, h * HEAD_DIM + 3 + t] = 1.0
    return selq, qc


def _mid(o, z, x, o_norm, w_out, post_gain, kv_norm, kv_w, kv_fbias, kv_k_norm,
         fox_pre, fox_w_in, fox_q_norm, *, tm=ROW_TILE):
    B, L, D = x.shape
    H, W = N_HEADS, N_HEADS * HEAD_DIM
    assert tm % SLAB == 0
    wo = w_out.astype(BF16)
    wkt = kv_w[:, :W].T.astype(BF16)
    kv_bf = kv_w.astype(BF16)
    wf = jnp.zeros((D, LANES), F32).at[:, :H].set(kv_w[:, 2 * W:2 * W + H]).astype(BF16)
    wft = jnp.zeros((2 * H, D), F32).at[:H, :].set(kv_w[:, 2 * W:2 * W + H].T).astype(BF16)
    fb = jnp.zeros((1, LANES), F32).at[0, :H].set(kv_fbias.astype(F32))
    fbr = jnp.zeros((2 * H, tm), F32).at[:H, :].set(
        jnp.broadcast_to(kv_fbias.astype(F32)[:, None], (H, tm)))
    kkn = jnp.broadcast_to(kv_k_norm.astype(F32)[:, None], (HEAD_DIM, tm))
    fox_bf = fox_w_in.astype(BF16)
    tri = np.tril(np.ones((tm, tm), np.float32))
    selq, qc = _aug_selectors()
    row = lambda n: pl.BlockSpec((1, tm, n), lambda b, i: (b, i, 0))
    head_spec = lambda n: pl.BlockSpec((1, H, tm, n), lambda b, i: (b, 0, i, 0))
    vec = lambda a, n: a.reshape(1, n).astype(F32)
    out_shape = (
        jax.ShapeDtypeStruct((B, L, D), F32),
        jax.ShapeDtypeStruct((B, H, L // SLAB, 2 * HEAD_DIM, SLAB), BF16),
        jax.ShapeDtypeStruct((B, H, L, HEAD_DIM), BF16),
        jax.ShapeDtypeStruct((B, H, L, 2 * HEAD_DIM), BF16),
        jax.ShapeDtypeStruct((B, L, W), BF16),
    )
    return pl.pallas_call(
        functools.partial(_mid_kernel, tm=tm),
        grid=(B, L // tm),
        in_specs=[
            head_spec(HEAD_DIM), row(W), row(D),
            _resident((1, HEAD_DIM)), _resident((W, D)), _resident((1, D)),
            _resident((1, D)), _resident((W, D)), _resident((D, W), 1), _resident((D, LANES)),
            _resident((2 * H, D)), _resident((1, LANES)), _resident((2 * H, tm)),
            _resident((HEAD_DIM, tm)),
            _resident((1, D)), _resident((D, W), 0), _resident((D, W), 1), _resident((1, HEAD_DIM)),
            _resident((tm, tm)), _resident((tm, tm)), _resident((LANES, W)), _resident((1, W)),
        ],
        out_specs=(row(D),
                   pl.BlockSpec((1, H, tm // SLAB, 2 * HEAD_DIM, SLAB), lambda b, i: (b, 0, i, 0, 0)),
                   head_spec(HEAD_DIM), head_spec(2 * HEAD_DIM), row(W)),
        out_shape=out_shape,
        scratch_shapes=[pltpu.VMEM((8, LANES), F32), pltpu.VMEM((2 * H, tm), F32)],
        compiler_params=pltpu.CompilerParams(
            dimension_semantics=("arbitrary", "arbitrary"),
            vmem_limit_bytes=VMEM_LIMIT_BYTES),
        name="mid",
    )(o, z, x, vec(o_norm, HEAD_DIM), wo, vec(post_gain, D),
      vec(kv_norm, D), wkt, kv_bf, wf, wft, fb, fbr, kkn,
      vec(fox_pre, D), fox_bf, fox_bf, vec(fox_q_norm, HEAD_DIM),
      jnp.asarray(tri, BF16), jnp.asarray(tri.T, BF16), jnp.asarray(selq, BF16), jnp.asarray(qc))


def _fox_attn_kernel(q_ref, kt_ref, v_ref, o_ref, acc_scr, m_scr, *, tq, hp):
    ts = SLAB
    i = pl.program_id(2)
    n_diag = tq // ts
    lane = lax.broadcasted_iota(jnp.int32, (ts, HEAD_DIM), 1)
    ones_col = jnp.where(lane == 0, 1.0, 0.0).astype(BF16)
    col = lax.broadcasted_iota(jnp.int32, (2 * ts, ts), 1)
    rowi = lax.broadcasted_iota(jnp.int32, (2 * ts, ts), 0)
    pair_masks = (col <= rowi, col + ts <= rowi)

    def v_aug(h, j):
        r0 = pl.multiple_of(j * ts, ts)
        return jnp.concatenate([v_ref[0, h, pl.ds(r0, ts), :], ones_col], axis=1)

    def scores(h, slabs, row0):
        q = q_ref[0, h, row0:tq, :]
        return [_dot(q, kt_ref[0, h, j]) for j in slabs]

    def finish(h, parts, slabs, row0, diag):
        if diag:
            parts = [jnp.concatenate([jnp.where(mk, p[:2 * ts], NEG_BIG), p[2 * ts:]], axis=0)
                     if p.shape[0] > 2 * ts else jnp.where(mk, p, NEG_BIG)
                     for p, mk in zip(parts, pair_masks)]
        mx = parts[0]
        for p in parts[1:]:
            mx = jnp.maximum(mx, p)
        m_old = m_scr[h, row0:tq, :]
        m_new = jnp.maximum(m_old, jnp.max(mx, axis=1, keepdims=True))
        m_scr[h, row0:tq, :] = m_new
        alpha = jnp.exp2(m_old - m_new)
        m_wide = jnp.concatenate([m_new, m_new], axis=1)
        pv = None
        for p, j in zip(parts, slabs):
            d = _dot(jnp.exp2(p - m_wide).astype(BF16), v_aug(h, j))
            pv = d if pv is None else pv + d
        acc_scr[h, row0:tq, :] = jnp.concatenate([alpha, alpha], axis=1) * acc_scr[h, row0:tq, :] + pv

    acc_scr[...] = jnp.zeros_like(acc_scr)
    m_scr[...] = jnp.full(m_scr.shape, NEG_BIG, F32)

    def step(slabs, row0, diag):
        parts = [scores(h, slabs, row0) for h in range(hp)]
        for h in range(hp):
            finish(h, parts[h], slabs, row0, diag)

    def full_blocks(jj, carry):
        step((2 * jj, 2 * jj + 1), 0, False)
        return carry

    lax.fori_loop(0, i * (n_diag // 2), full_blocks, 0)
    for d in range(n_diag // 2):
        step((i * n_diag + 2 * d, i * n_diag + 2 * d + 1), 2 * d * ts, True)
    for h in range(hp):
        acc = acc_scr[h]
        o_ref[0, h] = (acc[:, :HEAD_DIM] / acc[:, HEAD_DIM:HEAD_DIM + 1]).astype(o_ref.dtype)


def _fox_attn(q2, kt2, v, *, tq=1024, hp=4):
    B, H, L, Dh = v.shape
    tq = min(tq, L)
    assert tq % (2 * SLAB) == 0 and L % tq == 0 and H % hp == 0
    return pl.pallas_call(
        functools.partial(_fox_attn_kernel, tq=tq, hp=hp),
        grid=(B, H // hp, L // tq),
        in_specs=[pl.BlockSpec((1, hp, tq, 2 * Dh), lambda b, h, i: (b, h, i, 0)),
                  pl.BlockSpec((1, hp, L // SLAB, 2 * Dh, SLAB), lambda b, h, i: (b, h, 0, 0, 0)),
                  pl.BlockSpec((1, hp, L, Dh), lambda b, h, i: (b, h, 0, 0))],
        out_specs=pl.BlockSpec((1, hp, tq, Dh), lambda b, h, i: (b, h, i, 0)),
        out_shape=jax.ShapeDtypeStruct((B, H, L, Dh), BF16),
        scratch_shapes=[pltpu.VMEM((hp, tq, 2 * Dh), F32), pltpu.VMEM((hp, tq, LANES), F32)],
        compiler_params=pltpu.CompilerParams(
            dimension_semantics=("arbitrary", "arbitrary", "arbitrary"),
            vmem_limit_bytes=VMEM_LIMIT_BYTES),
        name="fox_attn",
    )(q2, kt2, v)


def _fox_out_kernel(o_ref, z_ref, h_ref, wo_ref, post_ref, out_ref):
    gated = []
    for h in range(N_HEADS):
        zh = z_ref[0, :, h * HEAD_DIM:(h + 1) * HEAD_DIM].astype(F32)
        gated.append((o_ref[0, h].astype(F32) * (zh * _sigmoid(zh))).astype(BF16))
    y = _dot(jnp.concatenate(gated, axis=1), wo_ref[...])
    out_ref[0] = h_ref[0] + _rms(y, post_ref[...])


def _fox_out(o, z2, h1, w_out, post_gain, *, tm=ROW_TILE):
    B, L, D = h1.shape
    H, W = N_HEADS, N_HEADS * HEAD_DIM
    row = lambda n: pl.BlockSpec((1, tm, n), lambda b, i: (b, i, 0))
    return pl.pallas_call(
        _fox_out_kernel,
        grid=(B, L // tm),
        in_specs=[pl.BlockSpec((1, H, tm, HEAD_DIM), lambda b, i: (b, 0, i, 0)), row(W), row(D),
                  _resident((W, D)), _resident((1, D))],
        out_specs=row(D),
        out_shape=jax.ShapeDtypeStruct((B, L, D), F32),
        compiler_params=pltpu.CompilerParams(
            dimension_semantics=("arbitrary", "arbitrary"),
            vmem_limit_bytes=VMEM_LIMIT_BYTES),
        name="fox_out",
    )(o, z2, h1, w_out.astype(BF16), post_gain.reshape(1, D).astype(F32))


def kernel(x, gdn_pre_norm, gdn_w_in, gdn_conv_w, gdn_a_log, gdn_dt_bias, gdn_o_norm, gdn_w_out, gdn_post_norm, kv_norm, kv_w, kv_forget_bias, kv_k_norm, fox_pre_norm, fox_w_in, fox_q_norm, fox_w_out, fox_post_norm):
    assert gdn_w_in.shape[0] == 1 and fox_w_in.shape[0] == 1, "one GDN layer then one FoX layer"
    q, k, v, z, gcol, grow = _gdn_in(x, gdn_pre_norm[0], gdn_w_in[0], gdn_conv_w[0],
                                     gdn_a_log[0], gdn_dt_bias[0])
    o = _gdn_scan(q, k, v, gcol, grow)
    h1, kt2, vs, q2, z2 = _mid(
        o, z, x, gdn_o_norm[0], gdn_w_out[0], gdn_post_norm[0],
        kv_norm, kv_w, kv_forget_bias, kv_k_norm, fox_pre_norm[0], fox_w_in[0], fox_q_norm[0])
    o2 = _fox_attn(q2, kt2, vs)
    return _fox_out(o2, z2, h1, fox_w_out[0], fox_post_norm[0])
```

```python
import functools
import math

import jax
import jax.numpy as jnp
import numpy as np
from jax import lax
from jax.experimental import pallas as pl
from jax.experimental.pallas import tpu as pltpu

HEAD_DIM = 128
N_HEADS = 8
CONV_K = 4
CHUNK = 64
GROUP = 1024
IN_TILE = 256
PAIR = 128
SCAN_HP = 8
SLAB = 256
ROW_TILE = 512
MID_SUB = 128
LOG2E = math.log2(math.e)
EPS = 1e-6
LANES = 128
VMEM_LIMIT_BYTES = 56 * 1024 * 1024
NEG_BIG = -1e30

F32 = jnp.float32
BF16 = jnp.bfloat16


def _dot(a, b):
    return jnp.dot(a, b, preferred_element_type=F32)


def _dot_nt(a, b):
    return lax.dot_general(a, b, (((1,), (1,)), ((), ())), preferred_element_type=F32)


def _dot_tn(a, b):
    return lax.dot_general(a, b, (((0,), (0,)), ((), ())), preferred_element_type=F32)


def _split3(x):
    hi = x.astype(BF16)
    r1 = x - hi.astype(F32)
    mid = r1.astype(BF16)
    lo = (r1 - mid.astype(F32)).astype(BF16)
    return hi, mid, lo


def _dot3(a_bf16, x):
    hi, mid, lo = _split3(x)
    return _dot(a_bf16, hi) + _dot(a_bf16, mid) + _dot(a_bf16, lo)


def _dot3_r(x, b_bf16):
    hi, mid, lo = _split3(x)
    return _dot(hi, b_bf16) + _dot(mid, b_bf16) + _dot(lo, b_bf16)


def _sigmoid(x):
    return 1.0 / (1.0 + jnp.exp(-x))


def _softplus(x):
    return jnp.maximum(x, 0.0) + jnp.log(1.0 + jnp.exp(-jnp.abs(x)))


def _silu(x):
    return x * _sigmoid(x)


def _l2(x):
    return x * lax.rsqrt(jnp.sum(x * x, axis=-1, keepdims=True) + EPS)


def _rms(x, gain):
    ms = jnp.mean(x * x, axis=-1, keepdims=True)
    return x * lax.rsqrt(ms + EPS) * gain


def _resident(shape, col_block=0):
    nd = len(shape)
    index = (0,) * (nd - 1) + (col_block,)
    return pl.BlockSpec(shape, lambda *_: index, pipeline_mode=pl.Buffered(1))


def _gdn_in_kernel(x_ref, gain_ref, w_ref, wab_ref, wabt_ref, convw_ref, hpc_ref, hpr_ref,
                   lt_ref, ut_ref,
                   q_ref, k_ref, v_ref, z_ref, gcol_ref, grow_ref,
                   xn_scr, pre_scr, carry_scr, *, tm, width, col_chunk):
    @pl.when(pl.program_id(1) == 0)
    def _():
        carry_scr[...] = jnp.zeros_like(carry_scr)

    xn_scr[...] = _rms(x_ref[0], gain_ref[...]).astype(BF16)
    xn = xn_scr[...]

    def z_chunk(c):
        lo = 3 * width + c * col_chunk
        z_ref[0, :, c * col_chunk:(c + 1) * col_chunk] = _dot(xn, w_ref[:, lo:lo + col_chunk]).astype(BF16)

    n_qkv = 3 * width // col_chunk
    n_z = width // col_chunk
    z_after = {(i + 1) * n_qkv // (n_z + 1) - 1: i for i in range(n_z)}

    def conv_chunk(c):
        lo = c * col_chunk
        pc = _dot(xn, w_ref[:, lo:lo + col_chunk])
        pre_scr[0:8, :] = carry_scr[:, lo:lo + col_chunk]
        pre_scr[8:8 + tm, :] = pc
        carry_scr[:, lo:lo + col_chunk] = pc[tm - 8:tm, :]
        cw = convw_ref[:, lo:lo + col_chunk]
        acc = pc * cw[CONV_K - 1:CONV_K, :]
        for j in range(1, CONV_K):
            acc = acc + pre_scr[8 - j:8 - j + tm, :] * cw[CONV_K - 1 - j:CONV_K - j, :]
        for hh in range(col_chunk // HEAD_DIM):
            col = lo + hh * HEAD_DIM
            out_ref = (q_ref, k_ref, v_ref)[col // width]
            out_ref[0, (col % width) // HEAD_DIM] = acc[:, hh * HEAD_DIM:(hh + 1) * HEAD_DIM]
        if c in z_after:
            z_chunk(z_after[c])

    for c in range(n_qkv // 2):
        conv_chunk(c)

    ab = _dot(xn, wab_ref[...])
    lane = lax.broadcasted_iota(jnp.int32, ab.shape, 1)
    g_col = jnp.where(lane < N_HEADS, hpc_ref[0:1, :] * _softplus(ab + hpc_ref[1:2, :]), 0.0)
    beta_col = _sigmoid(ab)
    cums = _dot3(lt_ref[...], g_col)
    gc_col = cums[0:tm]
    gl_col = pltpu.roll(cums[tm:2 * tm], 2 * N_HEADS, axis=1)
    gates = jnp.where(lane < N_HEADS, gc_col, jnp.where(lane < 2 * N_HEADS, beta_col, gl_col))
    for hg in range(N_HEADS // SCAN_HP):
        shift = (LANES - hg * SCAN_HP) % LANES
        gcol_ref[0, hg] = gates if shift == 0 else pltpu.roll(gates, shift, axis=1)

    abt = _dot_nt(wabt_ref[...], xn)
    row = lax.broadcasted_iota(jnp.int32, abt.shape, 0)
    g_row = jnp.where(row < N_HEADS, hpr_ref[0] * _softplus(abt + hpr_ref[1]), 0.0)
    gc_row = _dot3_r(g_row, ut_ref[...])
    for hg in range(N_HEADS // SCAN_HP):
        grow_ref[0, hg] = gc_row[hg * SCAN_HP:hg * SCAN_HP + 8, :]

    for c in range(n_qkv // 2, n_qkv):
        conv_chunk(c)


def _chunk_tri(tm):
    t = np.arange(tm)
    same = (t[:, None] // CHUNK) == (t[None, :] // CHUNK)
    lower = same & (t[None, :] <= t[:, None])
    return lower.astype(np.float32), same.astype(np.float32)


def _gdn_in(x, pre_gain, w_in, conv_w, a_log, dt_bias, *, tm=IN_TILE, col_chunk=256):
    B, L, D = x.shape
    W = N_HEADS * HEAD_DIM
    H = N_HEADS
    w_main = w_in.astype(BF16)
    wab = jnp.zeros((D, LANES), F32).at[:, :2 * H].set(w_in[:, 4 * W:4 * W + 2 * H]).astype(BF16)
    wabt = w_in[:, 4 * W:4 * W + 2 * H].T.astype(BF16)
    neg_a = -jnp.exp(a_log.astype(F32))
    hpc = jnp.zeros((2, LANES), F32).at[0, :H].set(neg_a).at[1, :H].set(dt_bias.astype(F32))
    hpr = jnp.zeros((2, 2 * H, tm), F32)
    hpr = hpr.at[0, :H, :].set(jnp.broadcast_to(neg_a[:, None], (H, tm)))
    hpr = hpr.at[1, :H, :].set(jnp.broadcast_to(dt_bias.astype(F32)[:, None], (H, tm)))
    lower, same = _chunk_tri(tm)
    lt = jnp.asarray(np.concatenate([lower, same], axis=0), BF16)
    ut = jnp.asarray(lower.T, BF16)

    kern = functools.partial(_gdn_in_kernel, tm=tm, width=W, col_chunk=col_chunk)
    row_spec = lambda n: pl.BlockSpec((1, tm, n), lambda b, i: (b, i, 0))
    head_spec = pl.BlockSpec((1, H, tm, HEAD_DIM), lambda b, i: (b, 0, i, 0))
    out_shape = (
        jax.ShapeDtypeStruct((B, H, L, HEAD_DIM), F32),
        jax.ShapeDtypeStruct((B, H, L, HEAD_DIM), F32),
        jax.ShapeDtypeStruct((B, H, L, HEAD_DIM), F32),
        jax.ShapeDtypeStruct((B, L, W), BF16),
        jax.ShapeDtypeStruct((B, H // SCAN_HP, L, LANES), F32),
        jax.ShapeDtypeStruct((B, H // SCAN_HP, 8, L), F32),
    )
    return pl.pallas_call(
        kern,
        grid=(B, L // tm),
        in_specs=[
            row_spec(D),
            _resident((1, D)),
            _resident((D, 4 * W)),
            _resident((D, LANES)),
            _resident((2 * H, D)),
            _resident((CONV_K, 3 * W)),
            _resident((2, LANES)),
            _resident((2, 2 * H, tm)),
            _resident((2 * tm, tm)),
            _resident((tm, tm)),
        ],
        out_specs=(head_spec, head_spec, head_spec, row_spec(W),
                   pl.BlockSpec((1, H // SCAN_HP, tm, LANES), lambda b, i: (b, 0, i, 0)),
                   pl.BlockSpec((1, H // SCAN_HP, 8, tm), lambda b, i: (b, 0, 0, i))),
        out_shape=out_shape,
        scratch_shapes=[
            pltpu.VMEM((tm, D), BF16),
            pltpu.VMEM((tm + 8, col_chunk), F32),
            pltpu.VMEM((8, 3 * W), F32),
        ],
        compiler_params=pltpu.CompilerParams(
            dimension_semantics=("arbitrary", "arbitrary"),
            vmem_limit_bytes=VMEM_LIMIT_BYTES),
        name="gdn_in",
    )(x, pre_gain.reshape(1, D).astype(F32), w_main, wab, wabt, conv_w.astype(F32), hpc, hpr, lt, ut)


def _gdn_scan_kernel(q_ref, k_ref, v_ref, gcol_ref, grow_ref, cmask_ref, smask_ref, eye_ref,
                     o_ref, s_scr, *, hp):
    P = PAIR
    n_pair = GROUP // P
    cpp = P // CHUNK

    @pl.when(pl.program_id(2) == 0)
    def _():
        s_scr[...] = jnp.zeros_like(s_scr)

    gcol = gcol_ref[0, 0]
    grow = grow_ref[0, 0]
    cmask = cmask_ref[...] > 0
    smask = smask_ref[...]
    eye = eye_ref[...]

    heads = range(hp)

    def setup(j):
        rows = slice(j * P, (j + 1) * P)
        st = dict(glb=[], kbg=[], vb=[], qg=[], kdec=[], attn=[], x=[], p=[])
        for h in heads:
            qf = _l2(_silu(q_ref[0, h, rows, :])) * (HEAD_DIM ** -0.5)
            kf = _l2(_silu(k_ref[0, h, rows, :]))
            vf = _silu(v_ref[0, h, rows, :])
            qh, kh = qf.astype(BF16), kf.astype(BF16)
            gc = jnp.broadcast_to(gcol[rows, h:h + 1], (P, LANES))
            beta = jnp.broadcast_to(gcol[rows, N_HEADS + h:N_HEADS + h + 1], (P, LANES))
            gl = jnp.broadcast_to(gcol[rows, 2 * N_HEADS + h:2 * N_HEADS + h + 1], (P, LANES))
            kb = kf * beta
            egc = jnp.exp(gc)
            st["glb"].append(gl)
            st["kbg"].append((kb * egc).astype(BF16))
            st["vb"].append((vf * beta).astype(BF16))
            st["qg"].append((qf * egc).astype(BF16))
            st["kdec"].append((kf * jnp.exp(gl - gc)).astype(BF16))
            kk = _dot_nt(kb.astype(BF16), kh)
            qk = _dot_nt(qh, kh)
            diff = gc - grow[h:h + 1, rows]
            dec = jnp.exp(jnp.where(cmask, diff, NEG_BIG))
            st["attn"].append((qk * dec).astype(BF16))
            st["x"].append(-(kk * dec * smask))
            st["p"].append(eye + st["x"][-1])
        return st

    def invert(st):
        x, p = st["x"], st["p"]
        xb = [x[h].astype(BF16) for h in heads]
        x = [_dot(xb[h], xb[h]) for h in heads]
        yield
        for _ in range(4):
            xb = [x[h].astype(BF16) for h in heads]
            r = [_dot(jnp.concatenate([xb[h], p[h].astype(BF16)], axis=0), xb[h]) for h in heads]
            x = [r[h][:P] for h in heads]
            p = [p[h] + r[h][P:] for h in heads]
            yield
        p = [p[h] + _dot(p[h].astype(BF16), x[h].astype(BF16)) for h in heads]
        yield
        uw = [_dot(p[h].astype(BF16), jnp.concatenate([st["vb"][h], st["kbg"][h]], axis=1)) for h in heads]
        st["u"] = [uw[h][:, :HEAD_DIM] for h in heads]
        st["wm"] = [uw[h][:, HEAD_DIM:].astype(BF16) for h in heads]
        yield

    def advance(st, j, s):
        vn_parts = [[] for _ in heads]
        for c in range(cpp):
            r0 = c * CHUNK
            wq = [_dot(jnp.concatenate([st["wm"][h][r0:r0 + CHUNK], st["qg"][h][r0:r0 + CHUNK]], axis=0),
                       s[h].astype(BF16)) for h in heads]
            yield
            vn = [(st["u"][h][r0:r0 + CHUNK] - wq[h][:CHUNK]).astype(BF16) for h in heads]
            pad = [jnp.zeros(((cpp - 1 - c) * CHUNK, HEAD_DIM), BF16)] if c < cpp - 1 else []
            for h in heads:
                vn_parts[h].append(vn[h])
                vn_all = jnp.concatenate(vn_parts[h] + pad, axis=0)
                o_c = wq[h][CHUNK:] + _dot(st["attn"][h][r0:r0 + CHUNK, :], vn_all)
                o_ref[0, h, j * P + r0:j * P + r0 + CHUNK, :] = o_c.astype(o_ref.dtype)
            yield
            for h in heads:
                s[h] = (s[h] * jnp.exp(st["glb"][h][r0:r0 + 1, :])
                        + _dot_tn(st["kdec"][h][r0:r0 + CHUNK], vn[h]))
            yield

    s = [s_scr[h] for h in heads]
    prev = None
    for j in range(n_pair):
        st = setup(j)
        gens = [invert(st)] + ([advance(prev, j - 1, s)] if prev is not None else [])
        while gens:
            for g in list(gens):
                if next(g, "done") == "done":
                    gens.remove(g)
        prev = st
    for _ in advance(prev, n_pair - 1, s):
        pass
    for h in heads:
        s_scr[h] = s[h]


def _pair_masks():
    t = np.arange(PAIR)
    same = (t[:, None] // CHUNK) == (t[None, :] // CHUNK)
    causal = same & (t[None, :] <= t[:, None])
    strict = same & (t[None, :] < t[:, None])
    return causal.astype(np.float32), strict.astype(np.float32), np.eye(PAIR, dtype=np.float32)


def _gdn_scan(q, k, v, gcol, grow):
    B, H, L, Dh = q.shape
    hp = SCAN_HP
    causal, strict, eye = _pair_masks()
    head_spec = pl.BlockSpec((1, hp, GROUP, Dh), lambda b, hg, g: (b, hg, g, 0))
    return pl.pallas_call(
        functools.partial(_gdn_scan_kernel, hp=hp),
        grid=(B, H // hp, L // GROUP),
        in_specs=[
            head_spec, head_spec, head_spec,
            pl.BlockSpec((1, 1, GROUP, LANES), lambda b, hg, g: (b, hg, g, 0)),
            pl.BlockSpec((1, 1, 8, GROUP), lambda b, hg, g: (b, hg, 0, g)),
            _resident((PAIR, PAIR)), _resident((PAIR, PAIR)), _resident((PAIR, PAIR)),
        ],
        out_specs=head_spec,
        out_shape=jax.ShapeDtypeStruct((B, H, L, Dh), BF16),
        scratch_shapes=[pltpu.VMEM((hp, Dh, Dh), F32)],
        compiler_params=pltpu.CompilerParams(
            dimension_semantics=("arbitrary", "arbitrary", "arbitrary"),
            vmem_limit_bytes=VMEM_LIMIT_BYTES),
        name="gdn_scan",
    )(q, k, v, gcol, grow, jnp.asarray(causal), jnp.asarray(strict), jnp.asarray(eye))


def _head_rms(xh, gain):
    ms = jnp.mean(xh * xh, axis=-1, keepdims=True)
    return xh * lax.rsqrt(ms + EPS) * gain


def _mid_kernel(o_ref, z_ref, x_ref, onorm_ref, wo_ref, post_ref,
                kvn_ref, wkt_ref, wv_ref, wf_ref, wft_ref, fb_ref, fbr_ref, kkn_ref,
                fpre_ref, wq_ref, wz_ref, qn_ref, lt_ref, ut_ref, selq_ref, qc_ref,
                h1_ref, kt_ref, v_ref, q_ref, z2_ref,
                carry_scr, carry_r_scr, *, tm):
    @pl.when(pl.program_id(1) == 0)
    def _():
        carry_scr[...] = jnp.zeros_like(carry_scr)
        carry_r_scr[...] = jnp.zeros_like(carry_r_scr)

    rb = MID_SUB
    n_split = tm // rb
    gated = []
    for r in range(n_split):
        rows = slice(r * rb, (r + 1) * rb)
        parts = []
        for h in range(N_HEADS):
            on = _head_rms(o_ref[0, h, rows, :].astype(F32), onorm_ref[...])
            zh = z_ref[0, rows, h * HEAD_DIM:(h + 1) * HEAD_DIM].astype(F32)
            parts.append((on * (zh * _sigmoid(zh))).astype(BF16))
        gated.append(jnp.concatenate(parts, axis=1))
    y = [_dot(gated[r], wo_ref[...]) for r in range(n_split)]
    u_parts, u2_parts = [], []
    for r in range(n_split):
        rows = slice(r * rb, (r + 1) * rb)
        h1_r = x_ref[0, rows, :] + _rms(y[r], post_ref[...])
        h1_ref[0, rows, :] = h1_r
        u_parts.append(_rms(h1_r, kvn_ref[...]).astype(BF16))
        u2_parts.append(_rms(h1_r, fpre_ref[...]).astype(BF16))

    u = jnp.concatenate(u_parts, axis=0)
    u2 = jnp.concatenate(u2_parts, axis=0)
    f = _dot(u, wf_ref[...]) + fb_ref[...]
    ft = _dot_nt(wft_ref[...], u) + fbr_ref[...]
    kt = _dot_nt(wkt_ref[...], u)
    vv = _dot(u, wv_ref[...])
    for h in range(N_HEADS):
        v_ref[0, h] = vv[:, h * HEAD_DIM:(h + 1) * HEAD_DIM].astype(BF16)

    lane = lax.broadcasted_iota(jnp.int32, f.shape, 1)
    logf = jnp.where(lane < N_HEADS, -_softplus(-f), 0.0)
    c = _dot3(lt_ref[...], logf) + carry_scr[0:1, :]
    carry_scr[...] = jnp.broadcast_to(c[tm - 1:tm, :], carry_scr.shape)
    c_hi, c_mid, c_lo = _split3(c * LOG2E)
    packed = (c_hi.astype(F32) + pltpu.roll(c_mid.astype(F32), N_HEADS, axis=1)
              + pltpu.roll(c_lo.astype(F32), 2 * N_HEADS, axis=1)).astype(BF16)
    qaug = _dot(packed, selq_ref[...]) + qc_ref[...]

    row = lax.broadcasted_iota(jnp.int32, ft.shape, 0)
    logf_r = jnp.where(row < N_HEADS, -_softplus(-ft), 0.0)
    c_r = _dot3_r(logf_r, ut_ref[...]) + carry_r_scr[...]
    carry_r_scr[...] = jnp.broadcast_to(c_r[:, tm - 1:tm], carry_r_scr.shape)
    r_hi, r_mid, r_lo = _split3(c_r * LOG2E)
    r_hi, r_mid, r_lo = r_hi.astype(F32), r_mid.astype(F32), r_lo.astype(F32)
    qq = _dot(u2, wq_ref[...])
    zz = _dot(u2, wz_ref[...])

    sub = lax.broadcasted_iota(jnp.int32, (16, tm), 0)
    for h in range(N_HEADS):
        kh = kt[h * HEAD_DIM:(h + 1) * HEAD_DIM, :]
        ms = jnp.mean(kh * kh, axis=0, keepdims=True)
        khn = (kh * lax.rsqrt(ms + EPS) * kkn_ref[...]).astype(BF16)
        aug = jnp.where(sub < 3, 1.0,
                        jnp.where(sub == 3, -r_hi[h:h + 1],
                                  jnp.where(sub == 4, -r_mid[h:h + 1],
                                            jnp.where(sub == 5, -r_lo[h:h + 1], 0.0)))).astype(BF16)
        for sl in range(tm // SLAB):
            cols = slice(sl * SLAB, (sl + 1) * SLAB)
            kt_ref[0, h, sl, 0:HEAD_DIM, :] = khn[:, cols]
            kt_ref[0, h, sl, HEAD_DIM:HEAD_DIM + 16, :] = aug[:, cols]
            kt_ref[0, h, sl, HEAD_DIM + 16:2 * HEAD_DIM, :] = jnp.zeros((HEAD_DIM - 16, SLAB), BF16)

    q_scale = HEAD_DIM ** -0.5 * LOG2E
    for h in range(N_HEADS):
        qh = _head_rms(qq[:, h * HEAD_DIM:(h + 1) * HEAD_DIM], qn_ref[...])
        q_ref[0, h, :, 0:HEAD_DIM] = (qh * q_scale).astype(BF16)
        q_ref[0, h, :, HEAD_DIM:2 * HEAD_DIM] = qaug[:, h * HEAD_DIM:(h + 1) * HEAD_DIM].astype(BF16)
    z2_ref[0] = zz.astype(BF16)


def _aug_selectors():
    selq = np.zeros((LANES, N_HEADS * HEAD_DIM), np.float32)
    qc = np.zeros((1, N_HEADS * HEAD_DIM), np.float32)
    for h in range(N_HEADS):
        for t in range(3):
            selq[t * N_HEADS + h, h * HEAD_DIM + t] = 1.0
            qc[0, h * HEAD_DIM + 3 + t] = 1.0
    return selq, qc


def _mid(o, z, x, o_norm, w_out, post_gain, kv_norm, kv_w, kv_fbias, kv_k_norm,
         fox_pre, fox_w_in, fox_q_norm, *, tm=ROW_TILE):
    B, L, D = x.shape
    H, W = N_HEADS, N_HEADS * HEAD_DIM
    assert tm % SLAB == 0
    wo = w_out.astype(BF16)
    wkt = kv_w[:, :W].T.astype(BF16)
    kv_bf = kv_w.astype(BF16)
    wf = jnp.zeros((D, LANES), F32).at[:, :H].set(kv_w[:, 2 * W:2 * W + H]).astype(BF16)
    wft = jnp.zeros((2 * H, D), F32).at[:H, :].set(kv_w[:, 2 * W:2 * W + H].T).astype(BF16)
    fb = jnp.zeros((1, LANES), F32).at[0, :H].set(kv_fbias.astype(F32))
    fbr = jnp.zeros((2 * H, tm), F32).at[:H, :].set(
        jnp.broadcast_to(kv_fbias.astype(F32)[:, None], (H, tm)))
    kkn = jnp.broadcast_to(kv_k_norm.astype(F32)[:, None], (HEAD_DIM, tm))
    fox_bf = fox_w_in.astype(BF16)
    tri = np.tril(np.ones((tm, tm), np.float32))
    selq, qc = _aug_selectors()
    row = lambda n: pl.BlockSpec((1, tm, n), lambda b, i: (b, i, 0))
    head_spec = lambda n: pl.BlockSpec((1, H, tm, n), lambda b, i: (b, 0, i, 0))
    vec = lambda a, n: a.reshape(1, n).astype(F32)
    out_shape = (
        jax.ShapeDtypeStruct((B, L, D), F32),
        jax.ShapeDtypeStruct((B, H, L // SLAB, 2 * HEAD_DIM, SLAB), BF16),
        jax.ShapeDtypeStruct((B, H, L, HEAD_DIM), BF16),
        jax.ShapeDtypeStruct((B, H, L, 2 * HEAD_DIM), BF16),
        jax.ShapeDtypeStruct((B, L, W), BF16),
    )
    return pl.pallas_call(
        functools.partial(_mid_kernel, tm=tm),
        grid=(B, L // tm),
        in_specs=[
            head_spec(HEAD_DIM), row(W), row(D),
            _resident((1, HEAD_DIM)), _resident((W, D)), _resident((1, D)),
            _resident((1, D)), _resident((W, D)), _resident((D, W), 1), _resident((D, LANES)),
            _resident((2 * H, D)), _resident((1, LANES)), _resident((2 * H, tm)),
            _resident((HEAD_DIM, tm)),
            _resident((1, D)), _resident((D, W), 0), _resident((D, W), 1), _resident((1, HEAD_DIM)),
            _resident((tm, tm)), _resident((tm, tm)), _resident((LANES, W)), _resident((1, W)),
        ],
        out_specs=(row(D),
                   pl.BlockSpec((1, H, tm // SLAB, 2 * HEAD_DIM, SLAB), lambda b, i: (b, 0, i, 0, 0)),
                   head_spec(HEAD_DIM), head_spec(2 * HEAD_DIM), row(W)),
        out_shape=out_shape,
        scratch_shapes=[pltpu.VMEM((8, LANES), F32), pltpu.VMEM((2 * H, tm), F32)],
        compiler_params=pltpu.CompilerParams(
            dimension_semantics=("arbitrary", "arbitrary"),
            vmem_limit_bytes=VMEM_LIMIT_BYTES),
        name="mid",
    )(o, z, x, vec(o_norm, HEAD_DIM), wo, vec(post_gain, D),
      vec(kv_norm, D), wkt, kv_bf, wf, wft, fb, fbr, kkn,
      vec(fox_pre, D), fox_bf, fox_bf, vec(fox_q_norm, HEAD_DIM),
      jnp.asarray(tri, BF16), jnp.asarray(tri.T, BF16), jnp.asarray(selq, BF16), jnp.asarray(qc))


def _fox_attn_kernel(q_ref, kt_ref, v_ref, z_ref, o_ref, acc_scr, m_scr, *, tq, hp):
    ts = SLAB
    i = pl.program_id(2)
    n_diag = tq // ts
    lane = lax.broadcasted_iota(jnp.int32, (ts, HEAD_DIM), 1)
    ones_col = jnp.where(lane == 0, 1.0, 0.0).astype(BF16)
    col = lax.broadcasted_iota(jnp.int32, (2 * ts, ts), 1)
    rowi = lax.broadcasted_iota(jnp.int32, (2 * ts, ts), 0)
    pair_masks = (col <= rowi, col + ts <= rowi)

    def v_aug(h, j):
        r0 = pl.multiple_of(j * ts, ts)
        return jnp.concatenate([v_ref[0, h, pl.ds(r0, ts), :], ones_col], axis=1)

    def scores(h, slabs, row0):
        q = q_ref[0, h, row0:tq, :]
        return [_dot(q, kt_ref[0, h, j]) for j in slabs]

    def finish(h, parts, slabs, row0, diag):
        if diag:
            parts = [jnp.concatenate([jnp.where(mk, p[:2 * ts], NEG_BIG), p[2 * ts:]], axis=0)
                     if p.shape[0] > 2 * ts else jnp.where(mk, p, NEG_BIG)
                     for p, mk in zip(parts, pair_masks)]
        mx = parts[0]
        for p in parts[1:]:
            mx = jnp.maximum(mx, p)
        m_old = m_scr[h, row0:tq, :]
        m_new = jnp.maximum(m_old, jnp.max(mx, axis=1, keepdims=True))
        m_scr[h, row0:tq, :] = m_new
        alpha = jnp.exp2(m_old - m_new)
        m_wide = jnp.concatenate([m_new, m_new], axis=1)
        pv = None
        for p, j in zip(parts, slabs):
            d = _dot(jnp.exp2(p - m_wide).astype(BF16), v_aug(h, j))
            pv = d if pv is None else pv + d
        acc_scr[h, row0:tq, :] = jnp.concatenate([alpha, alpha], axis=1) * acc_scr[h, row0:tq, :] + pv

    acc_scr[...] = jnp.zeros_like(acc_scr)
    m_scr[...] = jnp.full(m_scr.shape, NEG_BIG, F32)

    def step(slabs, row0, diag):
        parts = [scores(h, slabs, row0) for h in range(hp)]
        for h in range(hp):
            finish(h, parts[h], slabs, row0, diag)

    def full_blocks(jj, carry):
        step((2 * jj, 2 * jj + 1), 0, False)
        return carry

    lax.fori_loop(0, i * (n_diag // 2), full_blocks, 0)
    for d in range(n_diag // 2):
        step((i * n_diag + 2 * d, i * n_diag + 2 * d + 1), 2 * d * ts, True)
    for h in range(hp):
        acc = acc_scr[h]
        lanes = slice(h * HEAD_DIM, (h + 1) * HEAD_DIM)
        zh = z_ref[0, :, lanes].astype(F32)
        o = acc[:, :HEAD_DIM] / acc[:, HEAD_DIM:HEAD_DIM + 1]
        o_ref[0, :, lanes] = (o * (zh * _sigmoid(zh))).astype(o_ref.dtype)


def _fox_attn(q2, kt2, v, z2, *, tq=1024, hp=4):
    B, H, L, Dh = v.shape
    tq = min(tq, L)
    assert tq % (2 * SLAB) == 0 and L % tq == 0 and H % hp == 0
    row_block = pl.BlockSpec((1, tq, hp * Dh), lambda b, h, i: (b, i, h))
    return pl.pallas_call(
        functools.partial(_fox_attn_kernel, tq=tq, hp=hp),
        grid=(B, H // hp, L // tq),
        in_specs=[pl.BlockSpec((1, hp, tq, 2 * Dh), lambda b, h, i: (b, h, i, 0)),
                  pl.BlockSpec((1, hp, L // SLAB, 2 * Dh, SLAB), lambda b, h, i: (b, h, 0, 0, 0)),
                  pl.BlockSpec((1, hp, L, Dh), lambda b, h, i: (b, h, 0, 0)),
                  row_block],
        out_specs=row_block,
        out_shape=jax.ShapeDtypeStruct((B, L, H * Dh), BF16),
        scratch_shapes=[pltpu.VMEM((hp, tq, 2 * Dh), F32), pltpu.VMEM((hp, tq, LANES), F32)],
        compiler_params=pltpu.CompilerParams(
            dimension_semantics=("arbitrary", "arbitrary", "arbitrary"),
            vmem_limit_bytes=VMEM_LIMIT_BYTES),
        name="fox_attn",
    )(q2, kt2, v, z2)


def _fox_out_kernel(g_ref, h_ref, wo_ref, post_ref, out_ref):
    y = _dot(g_ref[0], wo_ref[...])
    out_ref[0] = h_ref[0] + _rms(y, post_ref[...])


def _fox_out(gated, h1, w_out, post_gain, *, tm=ROW_TILE):
    B, L, D = h1.shape
    H, W = N_HEADS, N_HEADS * HEAD_DIM
    row = lambda n: pl.BlockSpec((1, tm, n), lambda b, i: (b, i, 0))
    return pl.pallas_call(
        _fox_out_kernel,
        grid=(B, L // tm),
        in_specs=[row(W), row(D), _resident((W, D)), _resident((1, D))],
        out_specs=row(D),
        out_shape=jax.ShapeDtypeStruct((B, L, D), F32),
        compiler_params=pltpu.CompilerParams(
            dimension_semantics=("arbitrary", "arbitrary"),
            vmem_limit_bytes=VMEM_LIMIT_BYTES),
        name="fox_out",
    )(gated, h1, w_out.astype(BF16), post_gain.reshape(1, D).astype(F32))


def kernel(x, gdn_pre_norm, gdn_w_in, gdn_conv_w, gdn_a_log, gdn_dt_bias, gdn_o_norm, gdn_w_out, gdn_post_norm, kv_norm, kv_w, kv_forget_bias, kv_k_norm, fox_pre_norm, fox_w_in, fox_q_norm, fox_w_out, fox_post_norm):
    assert gdn_w_in.shape[0] == 1 and fox_w_in.shape[0] == 1, "one GDN layer then one FoX layer"
    q, k, v, z, gcol, grow = _gdn_in(x, gdn_pre_norm[0], gdn_w_in[0], gdn_conv_w[0],
                                     gdn_a_log[0], gdn_dt_bias[0])
    o = _gdn_scan(q, k, v, gcol, grow)
    h1, kt2, vs, q2, z2 = _mid(
        o, z, x, gdn_o_norm[0], gdn_w_out[0], gdn_post_norm[0],
        kv_norm, kv_w, kv_forget_bias, kv_k_norm, fox_pre_norm[0], fox_w_in[0], fox_q_norm[0])
    gated = _fox_attn(q2, kt2, vs, z2)
    return _fox_out(gated, h1, fox_w_out[0], fox_post_norm[0])
```

```python
import functools
import math

import jax
import jax.numpy as jnp
import numpy as np
from jax import lax
from jax.experimental import pallas as pl
from jax.experimental.pallas import tpu as pltpu

HEAD_DIM = 128
N_HEADS = 8
CONV_K = 4
CHUNK = 64
GROUP = 1024
IN_TILE = 256
PAIR = 128
SCAN_HP = 8
SLAB = 256
ROW_TILE = 512
MID_SUB = 128
LOG2E = math.log2(math.e)
EPS = 1e-6
LANES = 128
VMEM_LIMIT_BYTES = 56 * 1024 * 1024
NEG_BIG = -1e30

F32 = jnp.float32
BF16 = jnp.bfloat16


def _dot(a, b):
    return jnp.dot(a, b, preferred_element_type=F32)


def _dot_nt(a, b):
    return lax.dot_general(a, b, (((1,), (1,)), ((), ())), preferred_element_type=F32)


def _dot_tn(a, b):
    return lax.dot_general(a, b, (((0,), (0,)), ((), ())), preferred_element_type=F32)


def _split3(x):
    hi = x.astype(BF16)
    r1 = x - hi.astype(F32)
    mid = r1.astype(BF16)
    lo = (r1 - mid.astype(F32)).astype(BF16)
    return hi, mid, lo


def _dot3(a_bf16, x):
    hi, mid, lo = _split3(x)
    return _dot(a_bf16, hi) + _dot(a_bf16, mid) + _dot(a_bf16, lo)


def _dot3_r(x, b_bf16):
    hi, mid, lo = _split3(x)
    return _dot(hi, b_bf16) + _dot(mid, b_bf16) + _dot(lo, b_bf16)


def _sigmoid(x):
    return 1.0 / (1.0 + jnp.exp(-x))


def _softplus(x):
    return jnp.maximum(x, 0.0) + jnp.log(1.0 + jnp.exp(-jnp.abs(x)))


def _silu(x):
    return x * _sigmoid(x)


def _l2(x):
    return x * lax.rsqrt(jnp.sum(x * x, axis=-1, keepdims=True) + EPS)


def _rms(x, gain):
    ms = jnp.mean(x * x, axis=-1, keepdims=True)
    return x * lax.rsqrt(ms + EPS) * gain


def _resident(shape, col_block=0):
    nd = len(shape)
    index = (0,) * (nd - 1) + (col_block,)
    return pl.BlockSpec(shape, lambda *_: index, pipeline_mode=pl.Buffered(1))


def _gdn_in_kernel(x_ref, gain_ref, w_ref, wab_ref, wabt_ref, convw_ref, hpc_ref, hpr_ref,
                   lt_ref, ut_ref,
                   q_ref, k_ref, v_ref, z_ref, gcol_ref, grow_ref,
                   xn_scr, pre_scr, carry_scr, *, tm, width, col_chunk):
    @pl.when(pl.program_id(1) == 0)
    def _():
        carry_scr[...] = jnp.zeros_like(carry_scr)

    xn_scr[...] = _rms(x_ref[0], gain_ref[...]).astype(BF16)
    xn = xn_scr[...]

    def z_chunk(c):
        lo = 3 * width + c * col_chunk
        z_ref[0, :, c * col_chunk:(c + 1) * col_chunk] = _dot(xn, w_ref[:, lo:lo + col_chunk]).astype(BF16)

    n_qkv = 3 * width // col_chunk
    n_z = width // col_chunk
    z_after = {(i + 1) * n_qkv // (n_z + 1) - 1: i for i in range(n_z)}

    def conv_chunk(c):
        lo = c * col_chunk
        pc = _dot(xn, w_ref[:, lo:lo + col_chunk])
        pre_scr[0:8, :] = carry_scr[:, lo:lo + col_chunk]
        pre_scr[8:8 + tm, :] = pc
        carry_scr[:, lo:lo + col_chunk] = pc[tm - 8:tm, :]
        cw = convw_ref[:, lo:lo + col_chunk]
        acc = pc * cw[CONV_K - 1:CONV_K, :]
        for j in range(1, CONV_K):
            acc = acc + pre_scr[8 - j:8 - j + tm, :] * cw[CONV_K - 1 - j:CONV_K - j, :]
        for hh in range(col_chunk // HEAD_DIM):
            col = lo + hh * HEAD_DIM
            out_ref = (q_ref, k_ref, v_ref)[col // width]
            out_ref[0, (col % width) // HEAD_DIM] = acc[:, hh * HEAD_DIM:(hh + 1) * HEAD_DIM]
        if c in z_after:
            z_chunk(z_after[c])

    for c in range(n_qkv // 2):
        conv_chunk(c)

    ab = _dot(xn, wab_ref[...])
    lane = lax.broadcasted_iota(jnp.int32, ab.shape, 1)
    g_col = jnp.where(lane < N_HEADS, hpc_ref[0:1, :] * _softplus(ab + hpc_ref[1:2, :]), 0.0)
    beta_col = _sigmoid(ab)
    cums = _dot3(lt_ref[...], g_col)
    gc_col = cums[0:tm]
    gl_col = pltpu.roll(cums[tm:2 * tm], 2 * N_HEADS, axis=1)
    gates = jnp.where(lane < N_HEADS, gc_col, jnp.where(lane < 2 * N_HEADS, beta_col, gl_col))
    for hg in range(N_HEADS // SCAN_HP):
        shift = (LANES - hg * SCAN_HP) % LANES
        gcol_ref[0, hg] = gates if shift == 0 else pltpu.roll(gates, shift, axis=1)

    abt = _dot_nt(wabt_ref[...], xn)
    row = lax.broadcasted_iota(jnp.int32, abt.shape, 0)
    g_row = jnp.where(row < N_HEADS, hpr_ref[0] * _softplus(abt + hpr_ref[1]), 0.0)
    gc_row = _dot3_r(g_row, ut_ref[...])
    for hg in range(N_HEADS // SCAN_HP):
        grow_ref[0, hg] = gc_row[hg * SCAN_HP:hg * SCAN_HP + 8, :]

    for c in range(n_qkv // 2, n_qkv):
        conv_chunk(c)


def _chunk_tri(tm):
    t = np.arange(tm)
    same = (t[:, None] // CHUNK) == (t[None, :] // CHUNK)
    lower = same & (t[None, :] <= t[:, None])
    return lower.astype(np.float32), same.astype(np.float32)


def _gdn_in(x, pre_gain, w_in, conv_w, a_log, dt_bias, *, tm=IN_TILE, col_chunk=256):
    B, L, D = x.shape
    W = N_HEADS * HEAD_DIM
    H = N_HEADS
    w_main = w_in.astype(BF16)
    wab = jnp.zeros((D, LANES), F32).at[:, :2 * H].set(w_in[:, 4 * W:4 * W + 2 * H]).astype(BF16)
    wabt = w_in[:, 4 * W:4 * W + 2 * H].T.astype(BF16)
    neg_a = -jnp.exp(a_log.astype(F32))
    hpc = jnp.zeros((2, LANES), F32).at[0, :H].set(neg_a).at[1, :H].set(dt_bias.astype(F32))
    hpr = jnp.zeros((2, 2 * H, tm), F32)
    hpr = hpr.at[0, :H, :].set(jnp.broadcast_to(neg_a[:, None], (H, tm)))
    hpr = hpr.at[1, :H, :].set(jnp.broadcast_to(dt_bias.astype(F32)[:, None], (H, tm)))
    lower, same = _chunk_tri(tm)
    lt = jnp.asarray(np.concatenate([lower, same], axis=0), BF16)
    ut = jnp.asarray(lower.T, BF16)

    kern = functools.partial(_gdn_in_kernel, tm=tm, width=W, col_chunk=col_chunk)
    row_spec = lambda n: pl.BlockSpec((1, tm, n), lambda b, i: (b, i, 0))
    head_spec = pl.BlockSpec((1, H, tm, HEAD_DIM), lambda b, i: (b, 0, i, 0))
    out_shape = (
        jax.ShapeDtypeStruct((B, H, L, HEAD_DIM), F32),
        jax.ShapeDtypeStruct((B, H, L, HEAD_DIM), F32),
        jax.ShapeDtypeStruct((B, H, L, HEAD_DIM), F32),
        jax.ShapeDtypeStruct((B, L, W), BF16),
        jax.ShapeDtypeStruct((B, H // SCAN_HP, L, LANES), F32),
        jax.ShapeDtypeStruct((B, H // SCAN_HP, 8, L), F32),
    )
    return pl.pallas_call(
        kern,
        grid=(B, L // tm),
        in_specs=[
            row_spec(D),
            _resident((1, D)),
            _resident((D, 4 * W)),
            _resident((D, LANES)),
            _resident((2 * H, D)),
            _resident((CONV_K, 3 * W)),
            _resident((2, LANES)),
            _resident((2, 2 * H, tm)),
            _resident((2 * tm, tm)),
            _resident((tm, tm)),
        ],
        out_specs=(head_spec, head_spec, head_spec, row_spec(W),
                   pl.BlockSpec((1, H // SCAN_HP, tm, LANES), lambda b, i: (b, 0, i, 0)),
                   pl.BlockSpec((1, H // SCAN_HP, 8, tm), lambda b, i: (b, 0, 0, i))),
        out_shape=out_shape,
        scratch_shapes=[
            pltpu.VMEM((tm, D), BF16),
            pltpu.VMEM((tm + 8, col_chunk), F32),
            pltpu.VMEM((8, 3 * W), F32),
        ],
        compiler_params=pltpu.CompilerParams(
            dimension_semantics=("arbitrary", "arbitrary"),
            vmem_limit_bytes=VMEM_LIMIT_BYTES),
        name="gdn_in",
    )(x, pre_gain.reshape(1, D).astype(F32), w_main, wab, wabt, conv_w.astype(F32), hpc, hpr, lt, ut)


def _gdn_scan_kernel(q_ref, k_ref, v_ref, gcol_ref, grow_ref, cmask_ref, smask_ref, eye_ref,
                     o_ref, s_scr, *, hp):
    P = PAIR
    n_pair = GROUP // P
    cpp = P // CHUNK

    @pl.when(pl.program_id(2) == 0)
    def _():
        s_scr[...] = jnp.zeros_like(s_scr)

    gcol = gcol_ref[0, 0]
    grow = grow_ref[0, 0]
    cmask = cmask_ref[...] > 0
    smask = smask_ref[...]
    eye = eye_ref[...]

    heads = range(hp)

    def setup(j):
        rows = slice(j * P, (j + 1) * P)
        st = dict(glb=[], kbg=[], vb=[], qg=[], kdec=[], attn=[], x=[], p=[])
        for h in heads:
            qf = _l2(_silu(q_ref[0, h, rows, :])) * (HEAD_DIM ** -0.5)
            kf = _l2(_silu(k_ref[0, h, rows, :]))
            vf = _silu(v_ref[0, h, rows, :])
            qh, kh = qf.astype(BF16), kf.astype(BF16)
            gc = jnp.broadcast_to(gcol[rows, h:h + 1], (P, LANES))
            beta = jnp.broadcast_to(gcol[rows, N_HEADS + h:N_HEADS + h + 1], (P, LANES))
            gl = jnp.broadcast_to(gcol[rows, 2 * N_HEADS + h:2 * N_HEADS + h + 1], (P, LANES))
            kb = kf * beta
            egc = jnp.exp(gc)
            st["glb"].append(gl)
            st["kbg"].append((kb * egc).astype(BF16))
            st["vb"].append((vf * beta).astype(BF16))
            st["qg"].append((qf * egc).astype(BF16))
            st["kdec"].append((kf * jnp.exp(gl - gc)).astype(BF16))
            kk = _dot_nt(kb.astype(BF16), kh)
            qk = _dot_nt(qh, kh)
            diff = gc - grow[h:h + 1, rows]
            dec = jnp.exp(jnp.where(cmask, diff, NEG_BIG))
            st["attn"].append((qk * dec).astype(BF16))
            st["x"].append(-(kk * dec * smask))
            st["p"].append(eye + st["x"][-1])
        return st

    def invert(st):
        x, p = st["x"], st["p"]
        xb = [x[h].astype(BF16) for h in heads]
        x = [_dot(xb[h], xb[h]) for h in heads]
        yield
        for _ in range(4):
            xb = [x[h].astype(BF16) for h in heads]
            r = [_dot(jnp.concatenate([xb[h], p[h].astype(BF16)], axis=0), xb[h]) for h in heads]
            x = [r[h][:P] for h in heads]
            p = [p[h] + r[h][P:] for h in heads]
            yield
        p = [p[h] + _dot(p[h].astype(BF16), x[h].astype(BF16)) for h in heads]
        yield
        uw = [_dot(p[h].astype(BF16), jnp.concatenate([st["vb"][h], st["kbg"][h]], axis=1)) for h in heads]
        st["u"] = [uw[h][:, :HEAD_DIM] for h in heads]
        st["wm"] = [uw[h][:, HEAD_DIM:].astype(BF16) for h in heads]
        yield

    def advance(st, j, s):
        vn_parts = [[] for _ in heads]
        for c in range(cpp):
            r0 = c * CHUNK
            wq = [_dot(jnp.concatenate([st["wm"][h][r0:r0 + CHUNK], st["qg"][h][r0:r0 + CHUNK]], axis=0),
                       s[h].astype(BF16)) for h in heads]
            yield
            vn = [(st["u"][h][r0:r0 + CHUNK] - wq[h][:CHUNK]).astype(BF16) for h in heads]
            pad = [jnp.zeros(((cpp - 1 - c) * CHUNK, HEAD_DIM), BF16)] if c < cpp - 1 else []
            for h in heads:
                vn_parts[h].append(vn[h])
                vn_all = jnp.concatenate(vn_parts[h] + pad, axis=0)
                o_c = wq[h][CHUNK:] + _dot(st["attn"][h][r0:r0 + CHUNK, :], vn_all)
                o_ref[0, h, j * P + r0:j * P + r0 + CHUNK, :] = o_c.astype(o_ref.dtype)
            yield
            for h in heads:
                s[h] = (s[h] * jnp.exp(st["glb"][h][r0:r0 + 1, :])
                        + _dot_tn(st["kdec"][h][r0:r0 + CHUNK], vn[h]))
            yield

    s = [s_scr[h] for h in heads]
    prev = None
    for j in range(n_pair):
        st = setup(j)
        gens = [invert(st)] + ([advance(prev, j - 1, s)] if prev is not None else [])
        while gens:
            for g in list(gens):
                if next(g, "done") == "done":
                    gens.remove(g)
        prev = st
    for _ in advance(prev, n_pair - 1, s):
        pass
    for h in heads:
        s_scr[h] = s[h]


def _pair_masks():
    t = np.arange(PAIR)
    same = (t[:, None] // CHUNK) == (t[None, :] // CHUNK)
    causal = same & (t[None, :] <= t[:, None])
    strict = same & (t[None, :] < t[:, None])
    return causal.astype(np.float32), strict.astype(np.float32), np.eye(PAIR, dtype=np.float32)


def _gdn_scan(q, k, v, gcol, grow):
    B, H, L, Dh = q.shape
    hp = SCAN_HP
    causal, strict, eye = _pair_masks()
    head_spec = pl.BlockSpec((1, hp, GROUP, Dh), lambda b, hg, g: (b, hg, g, 0))
    return pl.pallas_call(
        functools.partial(_gdn_scan_kernel, hp=hp),
        grid=(B, H // hp, L // GROUP),
        in_specs=[
            head_spec, head_spec, head_spec,
            pl.BlockSpec((1, 1, GROUP, LANES), lambda b, hg, g: (b, hg, g, 0)),
            pl.BlockSpec((1, 1, 8, GROUP), lambda b, hg, g: (b, hg, 0, g)),
            _resident((PAIR, PAIR)), _resident((PAIR, PAIR)), _resident((PAIR, PAIR)),
        ],
        out_specs=head_spec,
        out_shape=jax.ShapeDtypeStruct((B, H, L, Dh), BF16),
        scratch_shapes=[pltpu.VMEM((hp, Dh, Dh), F32)],
        compiler_params=pltpu.CompilerParams(
            dimension_semantics=("arbitrary", "arbitrary", "arbitrary"),
            vmem_limit_bytes=VMEM_LIMIT_BYTES),
        name="gdn_scan",
    )(q, k, v, gcol, grow, jnp.asarray(causal), jnp.asarray(strict), jnp.asarray(eye))


def _head_rms(xh, gain):
    ms = jnp.mean(xh * xh, axis=-1, keepdims=True)
    return xh * lax.rsqrt(ms + EPS) * gain


def _mid_kernel(o_ref, z_ref, x_ref, onorm_ref, wo_ref, post_ref,
                kvn_ref, wkt_ref, wv_ref, wf_ref, wft_ref, fb_ref, fbr_ref, kkn_ref,
                fpre_ref, wq_ref, wz_ref, qn_ref, lt_ref, ut_ref, selq_ref, qc_ref,
                h1_ref, kt_ref, v_ref, q_ref, z2_ref,
                carry_scr, carry_r_scr, *, tm):
    @pl.when(pl.program_id(1) == 0)
    def _():
        carry_scr[...] = jnp.zeros_like(carry_scr)
        carry_r_scr[...] = jnp.zeros_like(carry_r_scr)

    rb = MID_SUB
    n_split = tm // rb
    gated = []
    for r in range(n_split):
        rows = slice(r * rb, (r + 1) * rb)
        parts = []
        for h in range(N_HEADS):
            on = _head_rms(o_ref[0, h, rows, :].astype(F32), onorm_ref[...])
            zh = z_ref[0, rows, h * HEAD_DIM:(h + 1) * HEAD_DIM].astype(F32)
            parts.append((on * (zh * _sigmoid(zh))).astype(BF16))
        gated.append(jnp.concatenate(parts, axis=1))
    y = [_dot(gated[r], wo_ref[...]) for r in range(n_split)]
    u_parts, u2_parts = [], []
    for r in range(n_split):
        rows = slice(r * rb, (r + 1) * rb)
        h1_r = x_ref[0, rows, :] + _rms(y[r], post_ref[...])
        h1_ref[0, rows, :] = h1_r
        u_parts.append(_rms(h1_r, kvn_ref[...]).astype(BF16))
        u2_parts.append(_rms(h1_r, fpre_ref[...]).astype(BF16))

    u = jnp.concatenate(u_parts, axis=0)
    u2 = jnp.concatenate(u2_parts, axis=0)
    f = _dot(u, wf_ref[...]) + fb_ref[...]
    ft = _dot_nt(wft_ref[...], u) + fbr_ref[...]
    kt = _dot_nt(wkt_ref[...], u)
    vv = _dot(u, wv_ref[...])
    for h in range(N_HEADS):
        v_ref[0, h] = vv[:, h * HEAD_DIM:(h + 1) * HEAD_DIM].astype(BF16)

    lane = lax.broadcasted_iota(jnp.int32, f.shape, 1)
    logf = jnp.where(lane < N_HEADS, -_softplus(-f), 0.0)
    c = _dot3(lt_ref[...], logf) + carry_scr[0:1, :]
    carry_scr[...] = jnp.broadcast_to(c[tm - 1:tm, :], carry_scr.shape)
    c_hi, c_mid, c_lo = _split3(c * LOG2E)
    packed = (c_hi.astype(F32) + pltpu.roll(c_mid.astype(F32), N_HEADS, axis=1)
              + pltpu.roll(c_lo.astype(F32), 2 * N_HEADS, axis=1)).astype(BF16)
    qaug = _dot(packed, selq_ref[...]) + qc_ref[...]

    row = lax.broadcasted_iota(jnp.int32, ft.shape, 0)
    logf_r = jnp.where(row < N_HEADS, -_softplus(-ft), 0.0)
    c_r = _dot3_r(logf_r, ut_ref[...]) + carry_r_scr[...]
    carry_r_scr[...] = jnp.broadcast_to(c_r[:, tm - 1:tm], carry_r_scr.shape)
    r_hi, r_mid, r_lo = _split3(c_r * LOG2E)
    r_hi, r_mid, r_lo = r_hi.astype(F32), r_mid.astype(F32), r_lo.astype(F32)
    qq = _dot(u2, wq_ref[...])
    zz = _dot(u2, wz_ref[...])

    sub = lax.broadcasted_iota(jnp.int32, (16, tm), 0)
    for h in range(N_HEADS):
        kh = kt[h * HEAD_DIM:(h + 1) * HEAD_DIM, :]
        ms = jnp.mean(kh * kh, axis=0, keepdims=True)
        khn = (kh * lax.rsqrt(ms + EPS) * kkn_ref[...]).astype(BF16)
        aug = jnp.where(sub < 3, 1.0,
                        jnp.where(sub == 3, -r_hi[h:h + 1],
                                  jnp.where(sub == 4, -r_mid[h:h + 1],
                                            jnp.where(sub == 5, -r_lo[h:h + 1], 0.0)))).astype(BF16)
        for sl in range(tm // SLAB):
            cols = slice(sl * SLAB, (sl + 1) * SLAB)
            kt_ref[0, h, sl, 0:HEAD_DIM, :] = khn[:, cols]
            kt_ref[0, h, sl, HEAD_DIM:HEAD_DIM + 16, :] = aug[:, cols]
            kt_ref[0, h, sl, HEAD_DIM + 16:2 * HEAD_DIM, :] = jnp.zeros((HEAD_DIM - 16, SLAB), BF16)

    q_scale = HEAD_DIM ** -0.5 * LOG2E
    for h in range(N_HEADS):
        qh = _head_rms(qq[:, h * HEAD_DIM:(h + 1) * HEAD_DIM], qn_ref[...])
        q_ref[0, h, :, 0:HEAD_DIM] = (qh * q_scale).astype(BF16)
        q_ref[0, h, :, HEAD_DIM:2 * HEAD_DIM] = qaug[:, h * HEAD_DIM:(h + 1) * HEAD_DIM].astype(BF16)
    z2_ref[0] = zz.astype(BF16)


def _aug_selectors():
    selq = np.zeros((LANES, N_HEADS * HEAD_DIM), np.float32)
    qc = np.zeros((1, N_HEADS * HEAD_DIM), np.float32)
    for h in range(N_HEADS):
        for t in range(3):
            selq[t * N_HEADS + h, h * HEAD_DIM + t] = 1.0
            qc[0, h * HEAD_DIM + 3 + t] = 1.0
    return selq, qc


def _mid(o, z, x, o_norm, w_out, post_gain, kv_norm, kv_w, kv_fbias, kv_k_norm,
         fox_pre, fox_w_in, fox_q_norm, *, tm=ROW_TILE):
    B, L, D = x.shape
    H, W = N_HEADS, N_HEADS * HEAD_DIM
    assert tm % SLAB == 0
    wo = w_out.astype(BF16)
    wkt = kv_w[:, :W].T.astype(BF16)
    kv_bf = kv_w.astype(BF16)
    wf = jnp.zeros((D, LANES), F32).at[:, :H].set(kv_w[:, 2 * W:2 * W + H]).astype(BF16)
    wft = jnp.zeros((2 * H, D), F32).at[:H, :].set(kv_w[:, 2 * W:2 * W + H].T).astype(BF16)
    fb = jnp.zeros((1, LANES), F32).at[0, :H].set(kv_fbias.astype(F32))
    fbr = jnp.zeros((2 * H, tm), F32).at[:H, :].set(
        jnp.broadcast_to(kv_fbias.astype(F32)[:, None], (H, tm)))
    kkn = jnp.broadcast_to(kv_k_norm.astype(F32)[:, None], (HEAD_DIM, tm))
    fox_bf = fox_w_in.astype(BF16)
    tri = np.tril(np.ones((tm, tm), np.float32))
    selq, qc = _aug_selectors()
    row = lambda n: pl.BlockSpec((1, tm, n), lambda b, i: (b, i, 0))
    head_spec = lambda n: pl.BlockSpec((1, H, tm, n), lambda b, i: (b, 0, i, 0))
    vec = lambda a, n: a.reshape(1, n).astype(F32)
    out_shape = (
        jax.ShapeDtypeStruct((B, L, D), F32),
        jax.ShapeDtypeStruct((B, H, L // SLAB, 2 * HEAD_DIM, SLAB), BF16),
        jax.ShapeDtypeStruct((B, H, L, HEAD_DIM), BF16),
        jax.ShapeDtypeStruct((B, H, L, 2 * HEAD_DIM), BF16),
        jax.ShapeDtypeStruct((B, L, W), BF16),
    )
    return pl.pallas_call(
        functools.partial(_mid_kernel, tm=tm),
        grid=(B, L // tm),
        in_specs=[
            head_spec(HEAD_DIM), row(W), row(D),
            _resident((1, HEAD_DIM)), _resident((W, D)), _resident((1, D)),
            _resident((1, D)), _resident((W, D)), _resident((D, W), 1), _resident((D, LANES)),
            _resident((2 * H, D)), _resident((1, LANES)), _resident((2 * H, tm)),
            _resident((HEAD_DIM, tm)),
            _resident((1, D)), _resident((D, W), 0), _resident((D, W), 1), _resident((1, HEAD_DIM)),
            _resident((tm, tm)), _resident((tm, tm)), _resident((LANES, W)), _resident((1, W)),
        ],
        out_specs=(row(D),
                   pl.BlockSpec((1, H, tm // SLAB, 2 * HEAD_DIM, SLAB), lambda b, i: (b, 0, i, 0, 0)),
                   head_spec(HEAD_DIM), head_spec(2 * HEAD_DIM), row(W)),
        out_shape=out_shape,
        scratch_shapes=[pltpu.VMEM((8, LANES), F32), pltpu.VMEM((2 * H, tm), F32)],
        compiler_params=pltpu.CompilerParams(
            dimension_semantics=("arbitrary", "arbitrary"),
            vmem_limit_bytes=VMEM_LIMIT_BYTES),
        name="mid",
    )(o, z, x, vec(o_norm, HEAD_DIM), wo, vec(post_gain, D),
      vec(kv_norm, D), wkt, kv_bf, wf, wft, fb, fbr, kkn,
      vec(fox_pre, D), fox_bf, fox_bf, vec(fox_q_norm, HEAD_DIM),
      jnp.asarray(tri, BF16), jnp.asarray(tri.T, BF16), jnp.asarray(selq, BF16), jnp.asarray(qc))


def _fox_attn_kernel(q_ref, kt_ref, v_ref, z_ref, o_ref, acc_scr, m_scr, *, tq, hp):
    ts = SLAB
    i = pl.program_id(2)
    n_diag = tq // ts
    lane = lax.broadcasted_iota(jnp.int32, (ts, HEAD_DIM), 1)
    ones_col = jnp.where(lane == 0, 1.0, 0.0).astype(BF16)
    col = lax.broadcasted_iota(jnp.int32, (2 * ts, ts), 1)
    rowi = lax.broadcasted_iota(jnp.int32, (2 * ts, ts), 0)
    pair_masks = (col <= rowi, col + ts <= rowi)

    def v_aug(h, j):
        r0 = pl.multiple_of(j * ts, ts)
        return jnp.concatenate([v_ref[0, h, pl.ds(r0, ts), :], ones_col], axis=1)

    def scores(h, slabs, row0):
        q = q_ref[0, h, row0:tq, :]
        return [_dot(q, kt_ref[0, h, j]) for j in slabs]

    def finish(h, parts, slabs, row0, diag):
        if diag:
            parts = [jnp.concatenate([jnp.where(mk, p[:2 * ts], NEG_BIG), p[2 * ts:]], axis=0)
                     if p.shape[0] > 2 * ts else jnp.where(mk, p, NEG_BIG)
                     for p, mk in zip(parts, pair_masks)]
        mx = parts[0]
        for p in parts[1:]:
            mx = jnp.maximum(mx, p)
        m_old = m_scr[h, row0:tq, :]
        m_new = jnp.maximum(m_old, jnp.max(mx, axis=1, keepdims=True))
        m_scr[h, row0:tq, :] = m_new
        alpha = jnp.exp2(m_old - m_new)
        m_wide = jnp.concatenate([m_new, m_new], axis=1)
        pv = None
        for p, j in zip(parts, slabs):
            d = _dot(jnp.exp2(p - m_wide).astype(BF16), v_aug(h, j))
            pv = d if pv is None else pv + d
        acc_scr[h, row0:tq, :] = jnp.concatenate([alpha, alpha], axis=1) * acc_scr[h, row0:tq, :] + pv

    acc_scr[...] = jnp.zeros_like(acc_scr)
    m_scr[...] = jnp.full(m_scr.shape, NEG_BIG, F32)

    def step(slabs, row0, diag):
        parts = [scores(h, slabs, row0) for h in range(hp)]
        for h in range(hp):
            finish(h, parts[h], slabs, row0, diag)

    def full_blocks(jj, carry):
        step((2 * jj, 2 * jj + 1), 0, False)
        return carry

    lax.fori_loop(0, i * (n_diag // 2), full_blocks, 0)
    for d in range(n_diag // 2):
        step((i * n_diag + 2 * d, i * n_diag + 2 * d + 1), 2 * d * ts, True)
    for h in range(hp):
        acc = acc_scr[h]
        lanes = slice(h * HEAD_DIM, (h + 1) * HEAD_DIM)
        zh = z_ref[0, :, lanes].astype(F32)
        o = acc[:, :HEAD_DIM] / acc[:, HEAD_DIM:HEAD_DIM + 1]
        o_ref[0, :, lanes] = (o * (zh * _sigmoid(zh))).astype(o_ref.dtype)


def _fox_attn(q2, kt2, v, z2, *, tq=1024, hp=4):
    B, H, L, Dh = v.shape
    tq = min(tq, L)
    assert tq % (2 * SLAB) == 0 and L % tq == 0 and H % hp == 0
    row_block = pl.BlockSpec((1, tq, hp * Dh), lambda b, h, i: (b, i, h))
    return pl.pallas_call(
        functools.partial(_fox_attn_kernel, tq=tq, hp=hp),
        grid=(B, H // hp, L // tq),
        in_specs=[pl.BlockSpec((1, hp, tq, 2 * Dh), lambda b, h, i: (b, h, i, 0)),
                  pl.BlockSpec((1, hp, L // SLAB, 2 * Dh, SLAB), lambda b, h, i: (b, h, 0, 0, 0)),
                  pl.BlockSpec((1, hp, L, Dh), lambda b, h, i: (b, h, 0, 0)),
                  row_block],
        out_specs=row_block,
        out_shape=jax.ShapeDtypeStruct((B, L, H * Dh), BF16),
        scratch_shapes=[pltpu.VMEM((hp, tq, 2 * Dh), F32), pltpu.VMEM((hp, tq, LANES), F32)],
        compiler_params=pltpu.CompilerParams(
            dimension_semantics=("arbitrary", "arbitrary", "arbitrary"),
            vmem_limit_bytes=VMEM_LIMIT_BYTES),
        name="fox_attn",
    )(q2, kt2, v, z2)


def _fox_out_kernel(g_ref, h_ref, wo_ref, post_ref, out_ref):
    y = _dot(g_ref[0], wo_ref[...])
    out_ref[0] = h_ref[0] + _rms(y, post_ref[...])


def _fox_out(gated, h1, w_out, post_gain, *, tm=2 * ROW_TILE):
    B, L, D = h1.shape
    H, W = N_HEADS, N_HEADS * HEAD_DIM
    row = lambda n: pl.BlockSpec((1, tm, n), lambda b, i: (b, i, 0))
    return pl.pallas_call(
        _fox_out_kernel,
        grid=(B, L // tm),
        in_specs=[row(W), row(D), _resident((W, D)), _resident((1, D))],
        out_specs=row(D),
        out_shape=jax.ShapeDtypeStruct((B, L, D), F32),
        compiler_params=pltpu.CompilerParams(
            dimension_semantics=("arbitrary", "arbitrary"),
            vmem_limit_bytes=VMEM_LIMIT_BYTES),
        name="fox_out",
    )(gated, h1, w_out.astype(BF16), post_gain.reshape(1, D).astype(F32))


def kernel(x, gdn_pre_norm, gdn_w_in, gdn_conv_w, gdn_a_log, gdn_dt_bias, gdn_o_norm, gdn_w_out, gdn_post_norm, kv_norm, kv_w, kv_forget_bias, kv_k_norm, fox_pre_norm, fox_w_in, fox_q_norm, fox_w_out, fox_post_norm):
    assert gdn_w_in.shape[0] == 1 and fox_w_in.shape[0] == 1, "one GDN layer then one FoX layer"
    q, k, v, z, gcol, grow = _gdn_in(x, gdn_pre_norm[0], gdn_w_in[0], gdn_conv_w[0],
                                     gdn_a_log[0], gdn_dt_bias[0])
    o = _gdn_scan(q, k, v, gcol, grow)
    h1, kt2, vs, q2, z2 = _mid(
        o, z, x, gdn_o_norm[0], gdn_w_out[0], gdn_post_norm[0],
        kv_norm, kv_w, kv_forget_bias, kv_k_norm, fox_pre_norm[0], fox_w_in[0], fox_q_norm[0])
    gated = _fox_attn(q2, kt2, vs, z2)
    return _fox_out(gated, h1, fox_w_out[0], fox_post_norm[0])
```

```python
import functools
import math

import jax
import jax.numpy as jnp
import numpy as np
from jax import lax
from jax.experimental import pallas as pl
from jax.experimental.pallas import tpu as pltpu

HEAD_DIM = 128
N_HEADS = 8
CONV_K = 4
CHUNK = 64
GROUP = 1024
IN_TILE = 256
PAIR = 128
SCAN_HP = 8
SLAB = 256
ROW_TILE = 512
MID_SUB = 128
LOG2E = math.log2(math.e)
EPS = 1e-6
LANES = 128
VMEM_LIMIT_BYTES = 56 * 1024 * 1024
NEG_BIG = -1e30

F32 = jnp.float32
BF16 = jnp.bfloat16


def _dot(a, b):
    return jnp.dot(a, b, preferred_element_type=F32)


def _dot_nt(a, b):
    return lax.dot_general(a, b, (((1,), (1,)), ((), ())), preferred_element_type=F32)


def _dot_tn(a, b):
    return lax.dot_general(a, b, (((0,), (0,)), ((), ())), preferred_element_type=F32)


def _split3(x):
    hi = x.astype(BF16)
    r1 = x - hi.astype(F32)
    mid = r1.astype(BF16)
    lo = (r1 - mid.astype(F32)).astype(BF16)
    return hi, mid, lo


def _dot3(a_bf16, x):
    hi, mid, lo = _split3(x)
    return _dot(a_bf16, hi) + _dot(a_bf16, mid) + _dot(a_bf16, lo)


def _dot3_r(x, b_bf16):
    hi, mid, lo = _split3(x)
    return _dot(hi, b_bf16) + _dot(mid, b_bf16) + _dot(lo, b_bf16)


def _sigmoid(x):
    return 1.0 / (1.0 + jnp.exp(-x))


def _softplus(x):
    return jnp.maximum(x, 0.0) + jnp.log(1.0 + jnp.exp(-jnp.abs(x)))


def _silu(x):
    return x * _sigmoid(x)


def _l2(x):
    return x * lax.rsqrt(jnp.sum(x * x, axis=-1, keepdims=True) + EPS)


def _rms(x, gain):
    ms = jnp.mean(x * x, axis=-1, keepdims=True)
    return x * lax.rsqrt(ms + EPS) * gain


def _resident(shape, col_block=0):
    nd = len(shape)
    index = (0,) * (nd - 1) + (col_block,)
    return pl.BlockSpec(shape, lambda *_: index, pipeline_mode=pl.Buffered(1))


def _gdn_in_kernel(x_ref, gain_ref, w_ref, wab_ref, wabt_ref, convw_ref, hpc_ref, hpr_ref,
                   lt_ref, ut_ref,
                   q_ref, k_ref, v_ref, z_ref, gcol_ref, grow_ref,
                   xn_scr, pre_scr, carry_scr, *, tm, width, col_chunk):
    @pl.when(pl.program_id(1) == 0)
    def _():
        carry_scr[...] = jnp.zeros_like(carry_scr)

    xn_scr[...] = _rms(x_ref[0], gain_ref[...]).astype(BF16)
    xn = xn_scr[...]

    def z_chunk(c):
        lo = 3 * width + c * col_chunk
        z_ref[0, :, c * col_chunk:(c + 1) * col_chunk] = _dot(xn, w_ref[:, lo:lo + col_chunk]).astype(BF16)

    n_qkv = 3 * width // col_chunk
    n_z = width // col_chunk
    z_after = {(i + 1) * n_qkv // (n_z + 1) - 1: i for i in range(n_z)}

    def conv_chunk(c):
        lo = c * col_chunk
        pc = _dot(xn, w_ref[:, lo:lo + col_chunk])
        pre_scr[0:8, :] = carry_scr[:, lo:lo + col_chunk]
        pre_scr[8:8 + tm, :] = pc
        carry_scr[:, lo:lo + col_chunk] = pc[tm - 8:tm, :]
        cw = convw_ref[:, lo:lo + col_chunk]
        acc = pc * cw[CONV_K - 1:CONV_K, :]
        for j in range(1, CONV_K):
            acc = acc + pre_scr[8 - j:8 - j + tm, :] * cw[CONV_K - 1 - j:CONV_K - j, :]
        for hh in range(col_chunk // HEAD_DIM):
            col = lo + hh * HEAD_DIM
            out_ref = (q_ref, k_ref, v_ref)[col // width]
            out_ref[0, (col % width) // HEAD_DIM] = acc[:, hh * HEAD_DIM:(hh + 1) * HEAD_DIM]
        if c in z_after:
            z_chunk(z_after[c])

    for c in range(n_qkv // 2):
        conv_chunk(c)

    ab = _dot(xn, wab_ref[...])
    lane = lax.broadcasted_iota(jnp.int32, ab.shape, 1)
    g_col = jnp.where(lane < N_HEADS, hpc_ref[0:1, :] * _softplus(ab + hpc_ref[1:2, :]), 0.0)
    beta_col = _sigmoid(ab)
    cums = _dot3(lt_ref[...], g_col)
    gc_col = cums[0:tm]
    gl_col = pltpu.roll(cums[tm:2 * tm], 2 * N_HEADS, axis=1)
    gates = jnp.where(lane < N_HEADS, gc_col, jnp.where(lane < 2 * N_HEADS, beta_col, gl_col))
    for hg in range(N_HEADS // SCAN_HP):
        shift = (LANES - hg * SCAN_HP) % LANES
        gcol_ref[0, hg] = gates if shift == 0 else pltpu.roll(gates, shift, axis=1)

    abt = _dot_nt(wabt_ref[...], xn)
    row = lax.broadcasted_iota(jnp.int32, abt.shape, 0)
    g_row = jnp.where(row < N_HEADS, hpr_ref[0] * _softplus(abt + hpr_ref[1]), 0.0)
    gc_row = _dot3_r(g_row, ut_ref[...])
    for hg in range(N_HEADS // SCAN_HP):
        grow_ref[0, hg] = gc_row[hg * SCAN_HP:hg * SCAN_HP + 8, :]

    for c in range(n_qkv // 2, n_qkv):
        conv_chunk(c)


def _chunk_tri(tm):
    t = np.arange(tm)
    same = (t[:, None] // CHUNK) == (t[None, :] // CHUNK)
    lower = same & (t[None, :] <= t[:, None])
    return lower.astype(np.float32), same.astype(np.float32)


def _gdn_in(x, pre_gain, w_in, conv_w, a_log, dt_bias, *, tm=IN_TILE, col_chunk=256):
    B, L, D = x.shape
    W = N_HEADS * HEAD_DIM
    H = N_HEADS
    w_main = w_in.astype(BF16)
    wab = jnp.zeros((D, LANES), F32).at[:, :2 * H].set(w_in[:, 4 * W:4 * W + 2 * H]).astype(BF16)
    wabt = w_in[:, 4 * W:4 * W + 2 * H].T.astype(BF16)
    neg_a = -jnp.exp(a_log.astype(F32))
    hpc = jnp.zeros((2, LANES), F32).at[0, :H].set(neg_a).at[1, :H].set(dt_bias.astype(F32))
    hpr = jnp.zeros((2, 2 * H, tm), F32)
    hpr = hpr.at[0, :H, :].set(jnp.broadcast_to(neg_a[:, None], (H, tm)))
    hpr = hpr.at[1, :H, :].set(jnp.broadcast_to(dt_bias.astype(F32)[:, None], (H, tm)))
    lower, same = _chunk_tri(tm)
    lt = jnp.asarray(np.concatenate([lower, same], axis=0), BF16)
    ut = jnp.asarray(lower.T, BF16)

    kern = functools.partial(_gdn_in_kernel, tm=tm, width=W, col_chunk=col_chunk)
    row_spec = lambda n: pl.BlockSpec((1, tm, n), lambda b, i: (b, i, 0))
    head_spec = pl.BlockSpec((1, H, tm, HEAD_DIM), lambda b, i: (b, 0, i, 0))
    out_shape = (
        jax.ShapeDtypeStruct((B, H, L, HEAD_DIM), F32),
        jax.ShapeDtypeStruct((B, H, L, HEAD_DIM), F32),
        jax.ShapeDtypeStruct((B, H, L, HEAD_DIM), F32),
        jax.ShapeDtypeStruct((B, L, W), BF16),
        jax.ShapeDtypeStruct((B, H // SCAN_HP, L, LANES), F32),
        jax.ShapeDtypeStruct((B, H // SCAN_HP, 8, L), F32),
    )
    return pl.pallas_call(
        kern,
        grid=(B, L // tm),
        in_specs=[
            row_spec(D),
            _resident((1, D)),
            _resident((D, 4 * W)),
            _resident((D, LANES)),
            _resident((2 * H, D)),
            _resident((CONV_K, 3 * W)),
            _resident((2, LANES)),
            _resident((2, 2 * H, tm)),
            _resident((2 * tm, tm)),
            _resident((tm, tm)),
        ],
        out_specs=(head_spec, head_spec, head_spec, row_spec(W),
                   pl.BlockSpec((1, H // SCAN_HP, tm, LANES), lambda b, i: (b, 0, i, 0)),
                   pl.BlockSpec((1, H // SCAN_HP, 8, tm), lambda b, i: (b, 0, 0, i))),
        out_shape=out_shape,
        scratch_shapes=[
            pltpu.VMEM((tm, D), BF16),
            pltpu.VMEM((tm + 8, col_chunk), F32),
            pltpu.VMEM((8, 3 * W), F32),
        ],
        compiler_params=pltpu.CompilerParams(
            dimension_semantics=("arbitrary", "arbitrary"),
            vmem_limit_bytes=VMEM_LIMIT_BYTES),
        name="gdn_in",
    )(x, pre_gain.reshape(1, D).astype(F32), w_main, wab, wabt, conv_w.astype(F32), hpc, hpr, lt, ut)


def _gdn_scan_kernel(q_ref, k_ref, v_ref, gcol_ref, grow_ref, cmask_ref, smask_ref, eye_ref,
                     o_ref, s_scr, *, hp):
    P = PAIR
    n_pair = GROUP // P
    cpp = P // CHUNK

    @pl.when(pl.program_id(2) == 0)
    def _():
        s_scr[...] = jnp.zeros_like(s_scr)

    gcol = gcol_ref[0, 0]
    grow = grow_ref[0, 0]
    cmask = cmask_ref[...] > 0
    smask = smask_ref[...]
    eye = eye_ref[...]

    heads = range(hp)

    def setup(j):
        rows = slice(j * P, (j + 1) * P)
        st = dict(glb=[], kbg=[], vb=[], qg=[], kdec=[], attn=[], x=[], p=[])
        for h in heads:
            qf = _l2(_silu(q_ref[0, h, rows, :])) * (HEAD_DIM ** -0.5)
            kf = _l2(_silu(k_ref[0, h, rows, :]))
            vf = _silu(v_ref[0, h, rows, :])
            qh, kh = qf.astype(BF16), kf.astype(BF16)
            gc = jnp.broadcast_to(gcol[rows, h:h + 1], (P, LANES))
            beta = jnp.broadcast_to(gcol[rows, N_HEADS + h:N_HEADS + h + 1], (P, LANES))
            gl = jnp.broadcast_to(gcol[rows, 2 * N_HEADS + h:2 * N_HEADS + h + 1], (P, LANES))
            kb = kf * beta
            egc = jnp.exp(gc)
            st["glb"].append(gl)
            st["kbg"].append((kb * egc).astype(BF16))
            st["vb"].append((vf * beta).astype(BF16))
            st["qg"].append((qf * egc).astype(BF16))
            st["kdec"].append((kf * jnp.exp(gl - gc)).astype(BF16))
            kk = _dot_nt(kb.astype(BF16), kh)
            qk = _dot_nt(qh, kh)
            diff = gc - grow[h:h + 1, rows]
            dec = jnp.exp(jnp.where(cmask, diff, NEG_BIG))
            st["attn"].append((qk * dec).astype(BF16))
            st["x"].append(-(kk * dec * smask))
            st["p"].append(eye + st["x"][-1])
        return st

    def invert(st):
        x, p = st["x"], st["p"]
        xb = [x[h].astype(BF16) for h in heads]
        x = [_dot(xb[h], xb[h]) for h in heads]
        yield
        for _ in range(4):
            xb = [x[h].astype(BF16) for h in heads]
            r = [_dot(jnp.concatenate([xb[h], p[h].astype(BF16)], axis=0), xb[h]) for h in heads]
            x = [r[h][:P] for h in heads]
            p = [p[h] + r[h][P:] for h in heads]
            yield
        p = [p[h] + _dot(p[h].astype(BF16), x[h].astype(BF16)) for h in heads]
        yield
        uw = [_dot(p[h].astype(BF16), jnp.concatenate([st["vb"][h], st["kbg"][h]], axis=1)) for h in heads]
        st["u"] = [uw[h][:, :HEAD_DIM] for h in heads]
        st["wm"] = [uw[h][:, HEAD_DIM:].astype(BF16) for h in heads]
        yield

    def advance(st, j, s):
        vn_parts = [[] for _ in heads]
        for c in range(cpp):
            r0 = c * CHUNK
            wq = [_dot(jnp.concatenate([st["wm"][h][r0:r0 + CHUNK], st["qg"][h][r0:r0 + CHUNK]], axis=0),
                       s[h].astype(BF16)) for h in heads]
            yield
            vn = [(st["u"][h][r0:r0 + CHUNK] - wq[h][:CHUNK]).astype(BF16) for h in heads]
            pad = [jnp.zeros(((cpp - 1 - c) * CHUNK, HEAD_DIM), BF16)] if c < cpp - 1 else []
            for h in heads:
                vn_parts[h].append(vn[h])
                vn_all = jnp.concatenate(vn_parts[h] + pad, axis=0)
                o_c = wq[h][CHUNK:] + _dot(st["attn"][h][r0:r0 + CHUNK, :], vn_all)
                o_ref[0, h, j * P + r0:j * P + r0 + CHUNK, :] = o_c.astype(o_ref.dtype)
            yield
            for h in heads:
                s[h] = (s[h] * jnp.exp(st["glb"][h][r0:r0 + 1, :])
                        + _dot_tn(st["kdec"][h][r0:r0 + CHUNK], vn[h]))
            yield

    s = [s_scr[h] for h in heads]
    prev = None
    for j in range(n_pair):
        st = setup(j)
        gens = [invert(st)] + ([advance(prev, j - 1, s)] if prev is not None else [])
        while gens:
            for g in list(gens):
                if next(g, "done") == "done":
                    gens.remove(g)
        prev = st
    for _ in advance(prev, n_pair - 1, s):
        pass
    for h in heads:
        s_scr[h] = s[h]


def _pair_masks():
    t = np.arange(PAIR)
    same = (t[:, None] // CHUNK) == (t[None, :] // CHUNK)
    causal = same & (t[None, :] <= t[:, None])
    strict = same & (t[None, :] < t[:, None])
    return causal.astype(np.float32), strict.astype(np.float32), np.eye(PAIR, dtype=np.float32)


def _gdn_scan(q, k, v, gcol, grow):
    B, H, L, Dh = q.shape
    hp = SCAN_HP
    causal, strict, eye = _pair_masks()
    head_spec = pl.BlockSpec((1, hp, GROUP, Dh), lambda b, hg, g: (b, hg, g, 0))
    return pl.pallas_call(
        functools.partial(_gdn_scan_kernel, hp=hp),
        grid=(B, H // hp, L // GROUP),
        in_specs=[
            head_spec, head_spec, head_spec,
            pl.BlockSpec((1, 1, GROUP, LANES), lambda b, hg, g: (b, hg, g, 0)),
            pl.BlockSpec((1, 1, 8, GROUP), lambda b, hg, g: (b, hg, 0, g)),
            _resident((PAIR, PAIR)), _resident((PAIR, PAIR)), _resident((PAIR, PAIR)),
        ],
        out_specs=head_spec,
        out_shape=jax.ShapeDtypeStruct((B, H, L, Dh), BF16),
        scratch_shapes=[pltpu.VMEM((hp, Dh, Dh), F32)],
        compiler_params=pltpu.CompilerParams(
            dimension_semantics=("arbitrary", "arbitrary", "arbitrary"),
            vmem_limit_bytes=VMEM_LIMIT_BYTES),
        name="gdn_scan",
    )(q, k, v, gcol, grow, jnp.asarray(causal), jnp.asarray(strict), jnp.asarray(eye))


def _head_rms(xh, gain):
    ms = jnp.mean(xh * xh, axis=-1, keepdims=True)
    return xh * lax.rsqrt(ms + EPS) * gain


def _mid_kernel(o_ref, z_ref, x_ref, onorm_ref, wo_ref, post_ref,
                kvn_ref, wkt_ref, wv_ref, wf_ref, wft_ref, fb_ref, fbr_ref, kkn_ref,
                fpre_ref, wq_ref, wz_ref, qn_ref, lt_ref, ut_ref, selq_ref, qc_ref,
                h1_ref, kt_ref, v_ref, q_ref, z2_ref,
                carry_scr, carry_r_scr, *, tm):
    @pl.when(pl.program_id(1) == 0)
    def _():
        carry_scr[...] = jnp.zeros_like(carry_scr)
        carry_r_scr[...] = jnp.zeros_like(carry_r_scr)

    rb = MID_SUB
    n_split = tm // rb
    gated = []
    for r in range(n_split):
        rows = slice(r * rb, (r + 1) * rb)
        parts = []
        for h in range(N_HEADS):
            on = _head_rms(o_ref[0, h, rows, :].astype(F32), onorm_ref[...])
            zh = z_ref[0, rows, h * HEAD_DIM:(h + 1) * HEAD_DIM].astype(F32)
            parts.append((on * (zh * _sigmoid(zh))).astype(BF16))
        gated.append(jnp.concatenate(parts, axis=1))
    y = [_dot(gated[r], wo_ref[...]) for r in range(n_split)]
    u_parts, u2_parts = [], []
    for r in range(n_split):
        rows = slice(r * rb, (r + 1) * rb)
        h1_r = x_ref[0, rows, :] + _rms(y[r], post_ref[...])
        h1_ref[0, rows, :] = h1_r
        u_parts.append(_rms(h1_r, kvn_ref[...]).astype(BF16))
        u2_parts.append(_rms(h1_r, fpre_ref[...]).astype(BF16))

    u = jnp.concatenate(u_parts, axis=0)
    u2 = jnp.concatenate(u2_parts, axis=0)
    f = _dot(u, wf_ref[...]) + fb_ref[...]
    ft = _dot_nt(wft_ref[...], u) + fbr_ref[...]
    kt = _dot_nt(wkt_ref[...], u)
    vv = _dot(u, wv_ref[...])
    for h in range(N_HEADS):
        v_ref[0, h] = vv[:, h * HEAD_DIM:(h + 1) * HEAD_DIM].astype(BF16)

    lane = lax.broadcasted_iota(jnp.int32, f.shape, 1)
    logf = jnp.where(lane < N_HEADS, -_softplus(-f), 0.0)
    c = _dot3(lt_ref[...], logf) + carry_scr[0:1, :]
    carry_scr[...] = jnp.broadcast_to(c[tm - 1:tm, :], carry_scr.shape)
    c_hi, c_mid, c_lo = _split3(c * LOG2E)
    packed = (c_hi.astype(F32) + pltpu.roll(c_mid.astype(F32), N_HEADS, axis=1)
              + pltpu.roll(c_lo.astype(F32), 2 * N_HEADS, axis=1)).astype(BF16)
    qaug = _dot(packed, selq_ref[...]) + qc_ref[...]

    row = lax.broadcasted_iota(jnp.int32, ft.shape, 0)
    logf_r = jnp.where(row < N_HEADS, -_softplus(-ft), 0.0)
    c_r = _dot3_r(logf_r, ut_ref[...]) + carry_r_scr[...]
    carry_r_scr[...] = jnp.broadcast_to(c_r[:, tm - 1:tm], carry_r_scr.shape)
    r_hi, r_mid, r_lo = _split3(c_r * LOG2E)
    r_hi, r_mid, r_lo = r_hi.astype(F32), r_mid.astype(F32), r_lo.astype(F32)
    qq = _dot(u2, wq_ref[...])
    zz = _dot(u2, wz_ref[...])

    sub = lax.broadcasted_iota(jnp.int32, (16, tm), 0)
    for h in range(N_HEADS):
        kh = kt[h * HEAD_DIM:(h + 1) * HEAD_DIM, :]
        ms = jnp.mean(kh * kh, axis=0, keepdims=True)
        khn = (kh * lax.rsqrt(ms + EPS) * kkn_ref[...]).astype(BF16)
        aug = jnp.where(sub < 3, 1.0,
                        jnp.where(sub == 3, -r_hi[h:h + 1],
                                  jnp.where(sub == 4, -r_mid[h:h + 1],
                                            jnp.where(sub == 5, -r_lo[h:h + 1], 0.0)))).astype(BF16)
        for sl in range(tm // SLAB):
            cols = slice(sl * SLAB, (sl + 1) * SLAB)
            kt_ref[0, h, sl, 0:HEAD_DIM, :] = khn[:, cols]
            kt_ref[0, h, sl, HEAD_DIM:HEAD_DIM + 16, :] = aug[:, cols]
            kt_ref[0, h, sl, HEAD_DIM + 16:2 * HEAD_DIM, :] = jnp.zeros((HEAD_DIM - 16, SLAB), BF16)

    q_scale = HEAD_DIM ** -0.5 * LOG2E
    for h in range(N_HEADS):
        qh = _head_rms(qq[:, h * HEAD_DIM:(h + 1) * HEAD_DIM], qn_ref[...])
        q_ref[0, h, :, 0:HEAD_DIM] = (qh * q_scale).astype(BF16)
        q_ref[0, h, :, HEAD_DIM:2 * HEAD_DIM] = qaug[:, h * HEAD_DIM:(h + 1) * HEAD_DIM].astype(BF16)
    z2_ref[0] = zz.astype(BF16)


def _aug_selectors():
    selq = np.zeros((LANES, N_HEADS * HEAD_DIM), np.float32)
    qc = np.zeros((1, N_HEADS * HEAD_DIM), np.float32)
    for h in range(N_HEADS):
        for t in range(3):
            selq[t * N_HEADS + h, h * HEAD_DIM + t] = 1.0
            qc[0, h * HEAD_DIM + 3 + t] = 1.0
    return selq, qc


def _mid(o, z, x, o_norm, w_out, post_gain, kv_norm, kv_w, kv_fbias, kv_k_norm,
         fox_pre, fox_w_in, fox_q_norm, *, tm=ROW_TILE):
    B, L, D = x.shape
    H, W = N_HEADS, N_HEADS * HEAD_DIM
    assert tm % SLAB == 0
    wo = w_out.astype(BF16)
    wkt = kv_w[:, :W].T.astype(BF16)
    kv_bf = kv_w.astype(BF16)
    wf = jnp.zeros((D, LANES), F32).at[:, :H].set(kv_w[:, 2 * W:2 * W + H]).astype(BF16)
    wft = jnp.zeros((2 * H, D), F32).at[:H, :].set(kv_w[:, 2 * W:2 * W + H].T).astype(BF16)
    fb = jnp.zeros((1, LANES), F32).at[0, :H].set(kv_fbias.astype(F32))
    fbr = jnp.zeros((2 * H, tm), F32).at[:H, :].set(
        jnp.broadcast_to(kv_fbias.astype(F32)[:, None], (H, tm)))
    kkn = jnp.broadcast_to(kv_k_norm.astype(F32)[:, None], (HEAD_DIM, tm))
    fox_bf = fox_w_in.astype(BF16)
    tri = np.tril(np.ones((tm, tm), np.float32))
    selq, qc = _aug_selectors()
    row = lambda n: pl.BlockSpec((1, tm, n), lambda b, i: (b, i, 0))
    head_spec = lambda n: pl.BlockSpec((1, H, tm, n), lambda b, i: (b, 0, i, 0))
    vec = lambda a, n: a.reshape(1, n).astype(F32)
    out_shape = (
        jax.ShapeDtypeStruct((B, L, D), F32),
        jax.ShapeDtypeStruct((B, H, L // SLAB, 2 * HEAD_DIM, SLAB), BF16),
        jax.ShapeDtypeStruct((B, H, L, HEAD_DIM), BF16),
        jax.ShapeDtypeStruct((B, H, L, 2 * HEAD_DIM), BF16),
        jax.ShapeDtypeStruct((B, L, W), BF16),
    )
    return pl.pallas_call(
        functools.partial(_mid_kernel, tm=tm),
        grid=(B, L // tm),
        in_specs=[
            head_spec(HEAD_DIM), row(W), row(D),
            _resident((1, HEAD_DIM)), _resident((W, D)), _resident((1, D)),
            _resident((1, D)), _resident((W, D)), _resident((D, W), 1), _resident((D, LANES)),
            _resident((2 * H, D)), _resident((1, LANES)), _resident((2 * H, tm)),
            _resident((HEAD_DIM, tm)),
            _resident((1, D)), _resident((D, W), 0), _resident((D, W), 1), _resident((1, HEAD_DIM)),
            _resident((tm, tm)), _resident((tm, tm)), _resident((LANES, W)), _resident((1, W)),
        ],
        out_specs=(row(D),
                   pl.BlockSpec((1, H, tm // SLAB, 2 * HEAD_DIM, SLAB), lambda b, i: (b, 0, i, 0, 0)),
                   head_spec(HEAD_DIM), head_spec(2 * HEAD_DIM), row(W)),
        out_shape=out_shape,
        scratch_shapes=[pltpu.VMEM((8, LANES), F32), pltpu.VMEM((2 * H, tm), F32)],
        compiler_params=pltpu.CompilerParams(
            dimension_semantics=("arbitrary", "arbitrary"),
            vmem_limit_bytes=VMEM_LIMIT_BYTES),
        name="mid",
    )(o, z, x, vec(o_norm, HEAD_DIM), wo, vec(post_gain, D),
      vec(kv_norm, D), wkt, kv_bf, wf, wft, fb, fbr, kkn,
      vec(fox_pre, D), fox_bf, fox_bf, vec(fox_q_norm, HEAD_DIM),
      jnp.asarray(tri, BF16), jnp.asarray(tri.T, BF16), jnp.asarray(selq, BF16), jnp.asarray(qc))


def _fox_attn_kernel(q_ref, kt_ref, v_ref, z_ref, o_ref, acc_scr, m_scr, *, tq, hp):
    ts = SLAB
    i = pl.program_id(2)
    n_diag = tq // ts
    lane = lax.broadcasted_iota(jnp.int32, (ts, HEAD_DIM), 1)
    ones_col = jnp.where(lane == 0, 1.0, 0.0).astype(BF16)
    col = lax.broadcasted_iota(jnp.int32, (2 * ts, ts), 1)
    rowi = lax.broadcasted_iota(jnp.int32, (2 * ts, ts), 0)
    pair_masks = (col <= rowi, col + ts <= rowi)

    def v_aug(h, j):
        r0 = pl.multiple_of(j * ts, ts)
        return jnp.concatenate([v_ref[0, h, pl.ds(r0, ts), :], ones_col], axis=1)

    def scores(h, slabs, row0):
        q = q_ref[0, h, row0:tq, :]
        return [_dot(q, kt_ref[0, h, j]) for j in slabs]

    def finish(h, parts, slabs, row0, diag):
        if diag:
            parts = [jnp.concatenate([jnp.where(mk, p[:2 * ts], NEG_BIG), p[2 * ts:]], axis=0)
                     if p.shape[0] > 2 * ts else jnp.where(mk, p, NEG_BIG)
                     for p, mk in zip(parts, pair_masks)]
        mx = parts[0]
        for p in parts[1:]:
            mx = jnp.maximum(mx, p)
        m_old = m_scr[h, row0:tq, :]
        m_new = jnp.maximum(m_old, jnp.max(mx, axis=1, keepdims=True))
        m_scr[h, row0:tq, :] = m_new
        alpha = jnp.exp2(m_old - m_new)
        m_wide = jnp.concatenate([m_new, m_new], axis=1)
        pv = None
        for p, j in zip(parts, slabs):
            d = _dot(jnp.exp2(p - m_wide).astype(BF16), v_aug(h, j))
            pv = d if pv is None else pv + d
        acc_scr[h, row0:tq, :] = jnp.concatenate([alpha, alpha], axis=1) * acc_scr[h, row0:tq, :] + pv

    acc_scr[...] = jnp.zeros_like(acc_scr)
    m_scr[...] = jnp.full(m_scr.shape, NEG_BIG, F32)

    def step(slabs, row0, diag):
        parts = [scores(h, slabs, row0) for h in range(hp)]
        for h in range(hp):
            finish(h, parts[h], slabs, row0, diag)

    def full_blocks(jj, carry):
        step((2 * jj, 2 * jj + 1), 0, False)
        return carry

    lax.fori_loop(0, i * (n_diag // 2), full_blocks, 0)
    for d in range(n_diag // 2):
        step((i * n_diag + 2 * d, i * n_diag + 2 * d + 1), 2 * d * ts, True)
    for h in range(hp):
        acc = acc_scr[h]
        lanes = slice(h * HEAD_DIM, (h + 1) * HEAD_DIM)
        zh = z_ref[0, :, lanes].astype(F32)
        o = acc[:, :HEAD_DIM] / acc[:, HEAD_DIM:HEAD_DIM + 1]
        o_ref[0, :, lanes] = (o * (zh * _sigmoid(zh))).astype(o_ref.dtype)


def _fox_attn(q2, kt2, v, z2, *, tq=1024, hp=4):
    B, H, L, Dh = v.shape
    tq = min(tq, L)
    assert tq % (2 * SLAB) == 0 and L % tq == 0 and H % hp == 0
    row_block = pl.BlockSpec((1, tq, hp * Dh), lambda b, h, i: (b, i, h))
    return pl.pallas_call(
        functools.partial(_fox_attn_kernel, tq=tq, hp=hp),
        grid=(B, H // hp, L // tq),
        in_specs=[pl.BlockSpec((1, hp, tq, 2 * Dh), lambda b, h, i: (b, h, i, 0)),
                  pl.BlockSpec((1, hp, L // SLAB, 2 * Dh, SLAB), lambda b, h, i: (b, h, 0, 0, 0)),
                  pl.BlockSpec((1, hp, L, Dh), lambda b, h, i: (b, h, 0, 0)),
                  row_block],
        out_specs=row_block,
        out_shape=jax.ShapeDtypeStruct((B, L, H * Dh), BF16),
        scratch_shapes=[pltpu.VMEM((hp, tq, 2 * Dh), F32), pltpu.VMEM((hp, tq, LANES), F32)],
        compiler_params=pltpu.CompilerParams(
            dimension_semantics=("arbitrary", "arbitrary", "arbitrary"),
            vmem_limit_bytes=VMEM_LIMIT_BYTES),
        name="fox_attn",
    )(q2, kt2, v, z2)


def _fox_out_kernel(g_ref, h_ref, wo_ref, post_ref, out_ref):
    y = _dot(g_ref[0], wo_ref[...])
    out_ref[0] = h_ref[0] + _rms(y, post_ref[...])


def _fox_out(gated, h1, w_out, post_gain, *, tm=4 * ROW_TILE):
    B, L, D = h1.shape
    H, W = N_HEADS, N_HEADS * HEAD_DIM
    row = lambda n: pl.BlockSpec((1, tm, n), lambda b, i: (b, i, 0))
    return pl.pallas_call(
        _fox_out_kernel,
        grid=(B, L // tm),
        in_specs=[row(W), row(D), _resident((W, D)), _resident((1, D))],
        out_specs=row(D),
        out_shape=jax.ShapeDtypeStruct((B, L, D), F32),
        compiler_params=pltpu.CompilerParams(
            dimension_semantics=("arbitrary", "arbitrary"),
            vmem_limit_bytes=VMEM_LIMIT_BYTES),
        name="fox_out",
    )(gated, h1, w_out.astype(BF16), post_gain.reshape(1, D).astype(F32))


def kernel(x, gdn_pre_norm, gdn_w_in, gdn_conv_w, gdn_a_log, gdn_dt_bias, gdn_o_norm, gdn_w_out, gdn_post_norm, kv_norm, kv_w, kv_forget_bias, kv_k_norm, fox_pre_norm, fox_w_in, fox_q_norm, fox_w_out, fox_post_norm):
    assert gdn_w_in.shape[0] == 1 and fox_w_in.shape[0] == 1, "one GDN layer then one FoX layer"
    q, k, v, z, gcol, grow = _gdn_in(x, gdn_pre_norm[0], gdn_w_in[0], gdn_conv_w[0],
                                     gdn_a_log[0], gdn_dt_bias[0])
    o = _gdn_scan(q, k, v, gcol, grow)
    h1, kt2, vs, q2, z2 = _mid(
        o, z, x, gdn_o_norm[0], gdn_w_out[0], gdn_post_norm[0],
        kv_norm, kv_w, kv_forget_bias, kv_k_norm, fox_pre_norm[0], fox_w_in[0], fox_q_norm[0])
    gated = _fox_attn(q2, kt2, vs, z2)
    return _fox_out(gated, h1, fox_w_out[0], fox_post_norm[0])
```
